```python
import jax, jax.numpy as jnp
from jax import lax
import numpy as np

D_MODEL = 4096
BATCH = 4
SEQ = 2048
DEPTH = 1
DEC_BATCH = 32
DEC_SEQ = 16
PAST_LEN = 1024

CHUNK = 64
SB_HEADS = 16
SB_HEAD_DIM = 128
SB_WIDTH = SB_HEADS * SB_HEAD_DIM
SB_SCALE = SB_HEAD_DIM ** -0.5
Q_BLOCK = 128
CONV_CH = D_MODEL // 2
CONV_WIDTH = 31
CONV_STATE = CONV_WIDTH - 1
N_IN = 3 * SB_WIDTH + 2 * CONV_CH + 2 * D_MODEL
N_MEM = 256
XA_HEADS = 4
XA_HEAD_DIM = 256
XA_WIDTH = XA_HEADS * XA_HEAD_DIM
XA_SCALE = XA_HEAD_DIM ** -0.5
N_EXPERTS = 256
N_GROUPS = 8
TOPK_GROUPS = 4
TOP_K = 8
D_EXPERT = 1024
ROUTED_SCALE = 2.5
EXPERT_BLOCK_MIN = 8
EXPERT_BLOCK_MAX = 64
DN_ALPHA = (2.0 * DEPTH) ** 0.25
DN_BETA = (8.0 * DEPTH) ** -0.25
LN_EPS = 1e-5

kernel_name = 'stick_breaking_conformer_moe_stream_step'


def _layer_norm(x, g, b):
    xf = x.astype(jnp.float32)
    mu = jnp.mean(xf, axis=-1, keepdims=True)
    xc = xf - mu
    var = jnp.mean(xc * xc, axis=-1, keepdims=True)
    return (xc * lax.rsqrt(var + LN_EPS) * g + b).astype(x.dtype)


def _sb_block(q_blk, k, v, q_pos):
    k_pos = jnp.arange(k.shape[1])
    z = jnp.einsum('bqhd,bkhd->bhqk', q_blk, k).astype(jnp.float32) * SB_SCALE
    before = k_pos[None, :] < q_pos[:, None]
    log_stay = jnp.where(before, jax.nn.log_sigmoid(-z), 0.0)
    between = lax.cumsum(log_stay, axis=3, reverse=True) - log_stay
    att = jnp.where(before, jnp.exp(jax.nn.log_sigmoid(z) + between), 0.0)
    return jnp.einsum('bhqk,bkhd->bqhd', att.astype(v.dtype), v)


def _stick_breaking(q, k, v, q_offset):
    b, tq, h, dh = q.shape
    q_pos = q_offset + jnp.arange(tq)
    if tq <= Q_BLOCK:
        return _sb_block(q, k, v, q_pos)
    nb = tq // Q_BLOCK
    q_blocks = q.reshape(b, nb, Q_BLOCK, h, dh).transpose(1, 0, 2, 3, 4)
    out = lax.map(lambda qp: _sb_block(qp[0], k, v, qp[1]),
                  (q_blocks, q_pos.reshape(nb, Q_BLOCK)))
    return out.transpose(1, 0, 2, 3, 4).reshape(b, tq, h, dh)


def _mixer_sublayer(x, k_past, v_past, conv_past, w_in, b_gate, conv_w, conv_b,
                    conv_ln_g, conv_ln_b, w_br_a, w_br_b, w_out, ln_g, ln_b):
    b, t, _ = x.shape
    offsets = [SB_WIDTH, 2 * SB_WIDTH, 3 * SB_WIDTH, 3 * SB_WIDTH + CONV_CH,
               3 * SB_WIDTH + 2 * CONV_CH, 3 * SB_WIDTH + 2 * CONV_CH + D_MODEL]
    q, k, v, u_val, u_gate, ga, gb = jnp.split(x @ w_in, offsets, axis=-1)
    q = q.reshape(b, t, SB_HEADS, SB_HEAD_DIM)
    k = k.reshape(b, t, SB_HEADS, SB_HEAD_DIM)
    v = v.reshape(b, t, SB_HEADS, SB_HEAD_DIM)
    if k_past is None:
        k_ctx, v_ctx, q_off = k, v, 0
        conv_past = jnp.zeros((b, CONV_STATE, CONV_CH), x.dtype)
    else:
        k_ctx = jnp.concatenate([k_past, k], axis=1)
        v_ctx = jnp.concatenate([v_past, v], axis=1)
        q_off = k_past.shape[1]
    o_sb = _stick_breaking(q, k_ctx, v_ctx, q_off).reshape(b, t, SB_WIDTH)
    u = u_val * jax.nn.sigmoid(u_gate)
    u_ctx = jnp.concatenate([conv_past.astype(u.dtype), u], axis=1)
    c = lax.conv_general_dilated(u_ctx, conv_w[:, None, :].astype(u.dtype), window_strides=(1,),
                                 padding='VALID', dimension_numbers=('NWC', 'WIO', 'NWC'),
                                 feature_group_count=CONV_CH) + conv_b
    c = jax.nn.silu(_layer_norm(c, conv_ln_g, conv_ln_b))
    gate_a = jax.nn.sigmoid(ga + b_gate[0])
    gate_b = jax.nn.sigmoid(gb + b_gate[1])
    merged = gate_a * (o_sb @ w_br_a) + gate_b * (c @ w_br_b)
    y = _layer_norm(DN_ALPHA * x + merged @ w_out, ln_g, ln_b)
    return y, k, v, u_ctx[:, -CONV_STATE:]


def _xattn_sublayer(x, mem_k, mem_v, wq, wo, ln_g, ln_b):
    b, t, _ = x.shape
    q = (x @ wq).reshape(b, t, XA_HEADS, XA_HEAD_DIM)
    s = jnp.einsum('bthd,bmhd->bhtm', q, mem_k).astype(jnp.float32) * XA_SCALE
    p = jax.nn.softmax(s, axis=-1).astype(mem_v.dtype)
    o = jnp.einsum('bhtm,bmhd->bthd', p, mem_v).reshape(b, t, XA_WIDTH)
    return _layer_norm(DN_ALPHA * x + o @ wo, ln_g, ln_b)


def _route(x2d, w_router, b_router):
    t = x2d.shape[0]
    scores = jax.nn.sigmoid((x2d @ w_router).astype(jnp.float32))
    choice = scores + b_router.astype(jnp.float32)
    grp = choice.reshape(t, N_GROUPS, N_EXPERTS // N_GROUPS)
    grp_score = jnp.sum(lax.top_k(grp, 2)[0], axis=-1)
    top_g = lax.top_k(grp_score, TOPK_GROUPS)[1]
    g_keep = jnp.any(top_g[:, :, None] == jnp.arange(N_GROUPS)[None, None, :], axis=1)
    e_keep = jnp.repeat(g_keep, N_EXPERTS // N_GROUPS, axis=1)
    idx = lax.top_k(jnp.where(e_keep, choice, -jnp.inf), TOP_K)[1]
    w = jnp.take_along_axis(scores, idx, axis=1)
    w = w / jnp.sum(w, axis=-1, keepdims=True) * ROUTED_SCALE
    return idx, w


def _expert_block_rows(n_assign):
    per_expert = max(n_assign // N_EXPERTS, 1)
    rows = 1 << (per_expert.bit_length() - 1)
    return min(EXPERT_BLOCK_MAX, max(EXPERT_BLOCK_MIN, rows))


def _routed_experts(x2d, idx, w, w_eg, w_eu, w_ed):
    t, d = x2d.shape
    n_assign = t * TOP_K
    blk = _expert_block_rows(n_assign)
    n_rows = (n_assign + N_EXPERTS * (blk - 1) + blk - 1) // blk * blk
    n_blocks = n_rows // blk
    flat_e = idx.reshape(-1)
    flat_t = jnp.arange(n_assign, dtype=jnp.int32) // TOP_K
    flat_w = w.reshape(-1).astype(x2d.dtype)
    order = jnp.argsort(flat_e)
    se = flat_e[order]
    counts = jnp.bincount(flat_e, length=N_EXPERTS)
    starts = jnp.cumsum(counts) - counts
    pcounts = (counts + blk - 1) // blk * blk
    pends = jnp.cumsum(pcounts)
    pstarts = pends - pcounts
    dest = pstarts[se] + jnp.arange(n_assign) - starts[se]
    row_tok = jnp.full((n_rows,), t, jnp.int32).at[dest].set(flat_t[order])
    row_w = jnp.zeros((n_rows,), x2d.dtype).at[dest].set(flat_w[order])
    blk_e = jnp.minimum(jnp.searchsorted(pends, jnp.arange(n_blocks) * blk, side='right'),
                        N_EXPERTS - 1)
    x_pad = jnp.concatenate([x2d, jnp.zeros((1, d), x2d.dtype)], axis=0)

    def body(acc, inp):
        rows, rw, e = inp
        xb = x_pad[rows]
        hb = jax.nn.silu(xb @ w_eg[e]) * (xb @ w_eu[e])
        return acc.at[rows].add((hb @ w_ed[e]) * rw[:, None]), None

    acc, _ = lax.scan(body, jnp.zeros((t + 1, d), x2d.dtype),
                      (row_tok.reshape(n_blocks, blk), row_w.reshape(n_blocks, blk), blk_e))
    return acc[:t]


def _moe_sublayer(x, w_router, b_router, w_eg, w_eu, w_ed, w_sg, w_su, w_sd, ln_g, ln_b):
    b, t, d = x.shape
    x2d = x.reshape(b * t, d)
    idx, w = _route(x2d, w_router, b_router)
    routed = _routed_experts(x2d, idx, w, w_eg, w_eu, w_ed)
    shared = (jax.nn.silu(x2d @ w_sg) * (x2d @ w_su)) @ w_sd
    return _layer_norm(DN_ALPHA * x + (routed + shared).reshape(b, t, d), ln_g, ln_b)


def setup_inputs(seed: int = 0) -> dict:
    key = jax.random.key(seed)
    ks = iter(jax.random.split(key, 48))

    def nrm(shape, scale=1.0):
        return jax.random.normal(next(ks), shape, jnp.float32) * scale

    def gain(shape):
        return 1.0 + nrm(shape, 0.02)

    L = DEPTH
    return {
        'x_prompt': nrm((BATCH, SEQ, D_MODEL)),
        'x_sample': nrm((DEC_BATCH, DEC_SEQ, D_MODEL)),
        'mem_prompt': nrm((BATCH, N_MEM, D_MODEL)),
        'cache_sb_k': nrm((L, DEC_BATCH, PAST_LEN, SB_HEADS, SB_HEAD_DIM)),
        'cache_sb_v': nrm((L, DEC_BATCH, PAST_LEN, SB_HEADS, SB_HEAD_DIM)),
        'state_conv': nrm((L, DEC_BATCH, CONV_STATE, CONV_CH), 0.5),
        'cache_mem_k': nrm((L, DEC_BATCH, N_MEM, XA_HEADS, XA_HEAD_DIM)),
        'cache_mem_v': nrm((L, DEC_BATCH, N_MEM, XA_HEADS, XA_HEAD_DIM)),
        'w_in': nrm((L, D_MODEL, N_IN), D_MODEL ** -0.5),
        'b_gate': nrm((L, 2, D_MODEL), 0.02),
        'conv_w': nrm((L, CONV_WIDTH, CONV_CH), CONV_WIDTH ** -0.5),
        'conv_b': nrm((L, CONV_CH), 0.02),
        'conv_ln_g': gain((L, CONV_CH)),
        'conv_ln_b': nrm((L, CONV_CH), 0.02),
        'w_br_a': nrm((L, SB_WIDTH, D_MODEL), SB_WIDTH ** -0.5),
        'w_br_b': nrm((L, CONV_CH, D_MODEL), CONV_CH ** -0.5),
        'w_out': nrm((L, D_MODEL, D_MODEL), D_MODEL ** -0.5 * DN_BETA),
        'ln1_g': gain((L, D_MODEL)),
        'ln1_b': nrm((L, D_MODEL), 0.02),
        'xa_wq': nrm((L, D_MODEL, XA_WIDTH), D_MODEL ** -0.5),
        'xa_wk': nrm((L, D_MODEL, XA_WIDTH), D_MODEL ** -0.5),
        'xa_wv': nrm((L, D_MODEL, XA_WIDTH), D_MODEL ** -0.5),
        'xa_wo': nrm((L, XA_WIDTH, D_MODEL), XA_WIDTH ** -0.5 * DN_BETA),
        'ln2_g': gain((L, D_MODEL)),
        'ln2_b': nrm((L, D_MODEL), 0.02),
        'w_router': nrm((L, D_MODEL, N_EXPERTS), D_MODEL ** -0.5),
        'b_router': nrm((L, N_EXPERTS), 0.01),
        'w_exp_gate': nrm((L, N_EXPERTS, D_MODEL, D_EXPERT), D_MODEL ** -0.5),
        'w_exp_up': nrm((L, N_EXPERTS, D_MODEL, D_EXPERT), D_MODEL ** -0.5),
        'w_exp_down': nrm((L, N_EXPERTS, D_EXPERT, D_MODEL), D_EXPERT ** -0.5 * DN_BETA),
        'w_sh_gate': nrm((L, D_MODEL, D_EXPERT), D_MODEL ** -0.5),
        'w_sh_up': nrm((L, D_MODEL, D_EXPERT), D_MODEL ** -0.5),
        'w_sh_down': nrm((L, D_EXPERT, D_MODEL), D_EXPERT ** -0.5 * DN_BETA),
        'ln3_g': gain((L, D_MODEL)),
        'ln3_b': nrm((L, D_MODEL), 0.02),
    }


def reference(x_prompt, x_sample, mem_prompt, cache_sb_k, cache_sb_v, state_conv,
              cache_mem_k, cache_mem_v, w_in, b_gate, conv_w, conv_b, conv_ln_g, conv_ln_b,
              w_br_a, w_br_b, w_out, ln1_g, ln1_b, xa_wq, xa_wk, xa_wv, xa_wo, ln2_g, ln2_b,
              w_router, b_router, w_exp_gate, w_exp_up, w_exp_down, w_sh_gate, w_sh_up,
              w_sh_down, ln3_g, ln3_b):
    hp, hs = x_prompt, x_sample
    kp_l, vp_l, cp_l, mkp_l, mvp_l, ks_l, vs_l, cs_l = [], [], [], [], [], [], [], []
    for l in range(DEPTH):
        mix_w = (w_in[l], b_gate[l], conv_w[l], conv_b[l], conv_ln_g[l], conv_ln_b[l],
                 w_br_a[l], w_br_b[l], w_out[l], ln1_g[l], ln1_b[l])
        moe_w = (w_router[l], b_router[l], w_exp_gate[l], w_exp_up[l], w_exp_down[l],
                 w_sh_gate[l], w_sh_up[l], w_sh_down[l], ln3_g[l], ln3_b[l])
        hp, kp, vp, cp = _mixer_sublayer(hp, None, None, None, *mix_w)
        nb = mem_prompt.shape[0]
        mkp = (mem_prompt @ xa_wk[l]).reshape(nb, N_MEM, XA_HEADS, XA_HEAD_DIM)
        mvp = (mem_prompt @ xa_wv[l]).reshape(nb, N_MEM, XA_HEADS, XA_HEAD_DIM)
        hp = _xattn_sublayer(hp, mkp, mvp, xa_wq[l], xa_wo[l], ln2_g[l], ln2_b[l])
        hp = _moe_sublayer(hp, *moe_w)
        hs, ks, vs, cs = _mixer_sublayer(hs, cache_sb_k[l], cache_sb_v[l], state_conv[l], *mix_w)
        hs = _xattn_sublayer(hs, cache_mem_k[l], cache_mem_v[l], xa_wq[l], xa_wo[l], ln2_g[l], ln2_b[l])
        hs = _moe_sublayer(hs, *moe_w)
        kp_l.append(kp); vp_l.append(vp); cp_l.append(cp); mkp_l.append(mkp); mvp_l.append(mvp)
        ks_l.append(ks); vs_l.append(vs); cs_l.append(cs)
    return (hp, hs, jnp.stack(kp_l), jnp.stack(vp_l), jnp.stack(cp_l), jnp.stack(mkp_l),
            jnp.stack(mvp_l), jnp.stack(ks_l), jnp.stack(vs_l), jnp.stack(cs_l))
```

```python
import functools

import jax
import jax.numpy as jnp
from jax import lax
from jax.experimental import pallas as pl
from jax.experimental.pallas import tpu as pltpu

F32 = jnp.float32
BF16 = jnp.bfloat16

V7X_VMEM_LIMIT_BYTES = 56 * 1024 * 1024
LANES = 128

SB_HEADS = 16
SB_HEAD_DIM = 128
XA_HEADS = 4
XA_HEAD_DIM = 256
CONV_WIDTH = 31
CONV_STATE = CONV_WIDTH - 1
CONV_HALO = 32
N_EXPERTS = 256
N_GROUPS = 8
TOPK_GROUPS = 4
TOP_K = 8
ROUTED_SCALE = 2.5
LN_EPS = 1e-5
EXPERT_TILE_ROWS = 512
COMBINE_TILE_TOKENS = 32


def _params(*sem):
    return pltpu.CompilerParams(dimension_semantics=sem, vmem_limit_bytes=V7X_VMEM_LIMIT_BYTES)


def _dot(a, b):
    return jnp.dot(a, b, preferred_element_type=F32)


def _dot_nt(a, b):
    return lax.dot_general(a, b, (((1,), (1,)), ((), ())), preferred_element_type=F32)


def _proj_kernel(*refs, n_w, n_bias, epilogue):
    x_ref = refs[0]
    w_refs = refs[1:1 + n_w]
    b_refs = refs[1 + n_w:1 + n_w + n_bias]
    o_refs = refs[1 + n_w + n_bias:]
    x = x_ref[...]
    accs = [_dot(x, w[...]) for w in w_refs]
    outs = epilogue(accs, [b[...] for b in b_refs])
    for o_ref, o in zip(o_refs, outs):
        o_ref[...] = o.astype(o_ref.dtype)


def _proj(x, weights, epilogue, out_dtypes, *, n_out, tm, tn, biases=(), name):
    m, k = x.shape
    tn = min(tn, n_out)
    grid = (m // tm, n_out // tn)
    in_specs = [pl.BlockSpec((tm, k), lambda i, j: (i, 0))]
    operands = [x]
    for w, off in weights:
        in_specs.append(pl.BlockSpec((k, tn), functools.partial(lambda i, j, o: (0, o + j), o=off // tn)))
        operands.append(w)
    for b, off in biases:
        in_specs.append(pl.BlockSpec((1, tn), functools.partial(lambda i, j, o: (0, o + j), o=off // tn)))
        operands.append(b)
    out_shape = [jax.ShapeDtypeStruct((m, n_out), dt) for dt in out_dtypes]
    out_specs = [pl.BlockSpec((tm, tn), lambda i, j: (i, j)) for _ in out_dtypes]
    return pl.pallas_call(
        functools.partial(_proj_kernel, n_w=len(weights), n_bias=len(biases), epilogue=epilogue),
        out_shape=out_shape, grid=grid, in_specs=in_specs, out_specs=out_specs,
        compiler_params=_params("parallel", "parallel"), name=name)(*operands)


def _ep_qkv(accs, _):
    q, k, v = accs
    return q, k, v, k, v


def _ep_glu(accs, _):
    val, gate = accs
    return (val * jax.nn.sigmoid(gate),)


def _ep_swiglu(accs, _):
    gate, up = accs
    return (jax.nn.silu(gate) * up,)


def _ep_copy(accs, _):
    return tuple(accs)


def _ep_kv(accs, _):
    k, v = accs
    return k, v, k, v


def _merge_kernel(x_ref, osb_ref, c_ref, wga_ref, wgb_ref, wa_ref, wb_ref, ba_ref, bb_ref, o_ref):
    x = x_ref[...]
    gate_a = jax.nn.sigmoid(_dot(x, wga_ref[...]) + ba_ref[...])
    gate_b = jax.nn.sigmoid(_dot(x, wgb_ref[...]) + bb_ref[...])
    br_a = _dot(osb_ref[...], wa_ref[...])
    br_b = _dot(c_ref[...], wb_ref[...])
    o_ref[...] = (gate_a * br_a + gate_b * br_b).astype(o_ref.dtype)


def _merge(xb, osb, cb, w_in, ga_col, gb_col, w_a, w_b, bias_a, bias_b, *, tm, tn):
    m, d = xb.shape
    ka = osb.shape[1]
    kb = cb.shape[1]
    n = w_a.shape[1]
    return pl.pallas_call(
        _merge_kernel,
        out_shape=jax.ShapeDtypeStruct((m, n), BF16),
        grid=(m // tm, n // tn),
        in_specs=[
            pl.BlockSpec((tm, d), lambda i, j: (i, 0)),
            pl.BlockSpec((tm, ka), lambda i, j: (i, 0)),
            pl.BlockSpec((tm, kb), lambda i, j: (i, 0)),
            pl.BlockSpec((d, tn), lambda i, j: (0, ga_col // tn + j)),
            pl.BlockSpec((d, tn), lambda i, j: (0, gb_col // tn + j)),
            pl.BlockSpec((ka, tn), lambda i, j: (0, j)),
            pl.BlockSpec((kb, tn), lambda i, j: (0, j)),
            pl.BlockSpec((1, tn), lambda i, j: (0, j)),
            pl.BlockSpec((1, tn), lambda i, j: (0, j)),
        ],
        out_specs=pl.BlockSpec((tm, tn), lambda i, j: (i, j)),
        compiler_params=_params("parallel", "parallel"), name="merge_gated_branches",
    )(xb, osb, cb, w_in, w_in, w_a, w_b, bias_a, bias_b)


LN_ROWS = 16


def _layer_norm_rows(x, g, b):
    mu = jnp.mean(x, axis=-1, keepdims=True)
    xc = x - mu
    var = jnp.mean(xc * xc, axis=-1, keepdims=True)
    return xc * lax.rsqrt(var + LN_EPS) * g + b


def _mm_res_ln_kernel(*refs, tn, alpha, has_extra):
    if has_extra:
        a_ref, w_ref, res_ref, extra_ref, g_ref, b_ref, o32_ref, o16_ref = refs
    else:
        a_ref, w_ref, res_ref, g_ref, b_ref, o32_ref, o16_ref = refs
    j = pl.program_id(1)
    n_col_tiles = o32_ref.shape[1] // tn
    pre = alpha * res_ref[...] + _dot(a_ref[...], w_ref[...])
    if has_extra:
        pre = pre + extra_ref[...]
    for jj in range(n_col_tiles):
        @pl.when(j == jj)
        def _():
            o32_ref[:, jj * tn:(jj + 1) * tn] = pre

    @pl.when(j == n_col_tiles - 1)
    def _():
        g = g_ref[...]
        b = b_ref[...]

        def body(r, carry):
            rows = pl.ds(pl.multiple_of(r * LN_ROWS, LN_ROWS), LN_ROWS)
            y = _layer_norm_rows(o32_ref[rows, :], g, b)
            o32_ref[rows, :] = y
            o16_ref[rows, :] = y.astype(o16_ref.dtype)
            return carry

        lax.fori_loop(0, o32_ref.shape[0] // LN_ROWS, body, 0)


def _mm_res_ln(a, w, res, g, b, *, alpha, tm, tn, extra=None):
    m, k = a.shape
    n = w.shape[1]
    in_specs = [
        pl.BlockSpec((tm, k), lambda i, j: (i, 0)),
        pl.BlockSpec((k, tn), lambda i, j: (0, j)),
        pl.BlockSpec((tm, tn), lambda i, j: (i, j)),
    ]
    operands = [a, w, res]
    if extra is not None:
        in_specs.append(pl.BlockSpec((tm, tn), lambda i, j: (i, j)))
        operands.append(extra)
    in_specs += [pl.BlockSpec((1, n), lambda i, j: (0, 0)), pl.BlockSpec((1, n), lambda i, j: (0, 0))]
    operands += [g, b]
    return pl.pallas_call(
        functools.partial(_mm_res_ln_kernel, tn=tn, alpha=alpha, has_extra=extra is not None),
        out_shape=[jax.ShapeDtypeStruct((m, n), F32), jax.ShapeDtypeStruct((m, n), BF16)],
        grid=(m // tm, n // tn), in_specs=in_specs,
        out_specs=[pl.BlockSpec((tm, n), lambda i, j: (i, 0)), pl.BlockSpec((tm, n), lambda i, j: (i, 0))],
        compiler_params=_params("parallel", "arbitrary"), name="matmul_residual_layernorm",
    )(*operands)


def _sb_tile(q, k, v, tri, before, carry, scale):
    z = _dot_nt(q, k) * scale
    softplus = jnp.maximum(z, 0.0) + jnp.log1p(jnp.exp(-jnp.abs(z)))
    log_stay = -softplus if before is None else jnp.where(before, -softplus, 0.0)
    hi = log_stay.astype(BF16)
    lo = (log_stay - hi.astype(F32)).astype(BF16)
    later = _dot(hi, tri) + _dot(lo, tri)
    arg = z + log_stay + later + carry
    if before is not None:
        arg = jnp.where(before, arg, -jnp.inf)
    att = jnp.exp(arg)
    out = _dot(att.astype(BF16), v)
    return out, carry + jnp.sum(log_stay, axis=1, keepdims=True)


def _later_key_matrix(tk):
    r = lax.broadcasted_iota(jnp.int32, (tk, tk), 0)
    c = lax.broadcasted_iota(jnp.int32, (tk, tk), 1)
    return (r > c).astype(BF16)


def _sb_prompt_kernel(q_ref, k_ref, v_ref, tri_ref, o_ref, *, tq, tk, scale):
    qi = pl.program_id(2)
    q = q_ref[...]
    tri = tri_ref[...]
    q_pos = qi * tq + lax.broadcasted_iota(jnp.int32, (tq, 1), 0)
    n_tiles = (qi + 1) * (tq // tk)

    def body(it, state):
        out, carry = state
        start = pl.multiple_of((n_tiles - 1 - it) * tk, tk)
        k_pos = start + lax.broadcasted_iota(jnp.int32, (1, tk), 1)
        o_t, carry = _sb_tile(q, k_ref[pl.ds(start, tk), :], v_ref[pl.ds(start, tk), :], tri,
                              k_pos < q_pos, carry, scale)
        return out + o_t, carry

    out, _ = lax.fori_loop(0, n_tiles, body,
                           (jnp.zeros((tq, q.shape[1]), F32), jnp.zeros((tq, 1), F32)))
    o_ref[...] = out.astype(o_ref.dtype)


def _sb_prompt(qb, kb, vb, *, batch, seq, heads, head_dim, tq, tk):
    nq = seq // tq
    return pl.pallas_call(
        functools.partial(_sb_prompt_kernel, tq=tq, tk=tk, scale=head_dim ** -0.5),
        out_shape=jax.ShapeDtypeStruct((batch * seq, heads * head_dim), BF16),
        grid=(batch, heads, nq),
        in_specs=[
            pl.BlockSpec((tq, head_dim), lambda b, h, i: (b * nq + i, h)),
            pl.BlockSpec((seq, head_dim), lambda b, h, i: (b, h)),
            pl.BlockSpec((seq, head_dim), lambda b, h, i: (b, h)),
            pl.BlockSpec((tk, tk), lambda b, h, i: (0, 0)),
        ],
        out_specs=pl.BlockSpec((tq, head_dim), lambda b, h, i: (b * nq + i, h)),
        compiler_params=_params("parallel", "parallel", "parallel"), name="stick_breaking_prompt",
    )(qb, kb, vb, _later_key_matrix(tk))


def _sb_sample_kernel(q_ref, kn_ref, vn_ref, kc_ref, vc_ref, tri_new_ref, tri_ref, o_ref, *,
                      heads, head_dim, tk, scale):
    t_new = q_ref.shape[0]
    past = kc_ref.shape[0]
    r = lax.broadcasted_iota(jnp.int32, (t_new, t_new), 0)
    c = lax.broadcasted_iota(jnp.int32, (t_new, t_new), 1)
    before_new = c < r
    tri_new = tri_new_ref[...]
    tri = tri_ref[...]

    for h in range(heads):
        cols = slice(h * head_dim, (h + 1) * head_dim)
        q = q_ref[:, cols]
        out, carry = _sb_tile(q, kn_ref[:, cols], vn_ref[:, cols], tri_new, before_new,
                              jnp.zeros((t_new, 1), F32), scale)
        for t in reversed(range(past // tk)):
            rows = slice(t * tk, (t + 1) * tk)
            o_t, carry = _sb_tile(q, kc_ref[rows, cols].astype(BF16), vc_ref[rows, cols].astype(BF16),
                                  tri, None, carry, scale)
            out = out + o_t
        o_ref[:, cols] = out.astype(o_ref.dtype)


def _sb_sample(qb, kb, vb, cache_k, cache_v, *, row0, batch, t_new, heads, head_dim, tk):
    width = heads * head_dim
    past = cache_k.shape[1]
    blk0 = row0 // t_new
    new_spec = pl.BlockSpec((t_new, width), lambda b: (blk0 + b, 0))
    cache_spec = pl.BlockSpec((None, past, width), lambda b: (b, 0, 0))
    return pl.pallas_call(
        functools.partial(_sb_sample_kernel, heads=heads, head_dim=head_dim, tk=tk, scale=head_dim ** -0.5),
        out_shape=jax.ShapeDtypeStruct((batch * t_new, width), BF16),
        grid=(batch,),
        in_specs=[new_spec, new_spec, new_spec, cache_spec, cache_spec,
                  pl.BlockSpec((t_new, t_new), lambda b: (0, 0)), pl.BlockSpec((tk, tk), lambda b: (0, 0))],
        out_specs=pl.BlockSpec((t_new, width), lambda b: (b, 0)),
        compiler_params=_params("parallel"), name="stick_breaking_sample",
    )(qb, kb, vb, cache_k, cache_v, _later_key_matrix(t_new), _later_key_matrix(tk))


CONV_ROWS = 16
CONV_LANES = 512


def _conv_kernel(halo_ref, u_ref, w_ref, cb_ref, g_ref, b_ref, o_ref, win_ref, shift_ref, pre_ref, *, tt):
    ch = u_ref.shape[1]
    win_ref[0:CONV_HALO, :] = halo_ref[...]
    win_ref[CONV_HALO:CONV_HALO + tt, :] = u_ref[...]
    n_shift_rows = shift_ref.shape[1]
    for s in range(1, 8):
        shift_ref[s - 1] = win_ref[s:s + n_shift_rows, :]

    def row_body(rc, carry):
        r0 = pl.multiple_of(rc * CONV_ROWS, CONV_ROWS)
        for lc in range(ch // CONV_LANES):
            lanes = slice(lc * CONV_LANES, (lc + 1) * CONV_LANES)
            acc = jnp.zeros((CONV_ROWS, CONV_LANES), F32)
            for j in range(CONV_WIDTH):
                off = j + CONV_HALO - CONV_STATE
                rows = pl.ds(r0 + (off // 8) * 8, CONV_ROWS)
                if off % 8 == 0:
                    x = win_ref[rows, lanes]
                else:
                    x = shift_ref[off % 8 - 1, rows, lanes]
                acc = acc + x * w_ref[j:j + 1, lanes]
            pre_ref[pl.ds(r0, CONV_ROWS), lanes] = acc + cb_ref[:, lanes]
        y = _layer_norm_rows(pre_ref[pl.ds(r0, CONV_ROWS), :], g_ref[...], b_ref[...])
        o_ref[pl.ds(r0, CONV_ROWS), :] = jax.nn.silu(y).astype(o_ref.dtype)
        return carry

    lax.fori_loop(0, tt // CONV_ROWS, row_body, 0)


def _conv_module(u_all, halo, conv_w, conv_b, ln_g, ln_b, *, row0, n_tiles, tt):
    ch = u_all.shape[1]
    blk0 = row0 // tt
    vec = pl.BlockSpec((1, ch), lambda i: (0, 0))
    return pl.pallas_call(
        functools.partial(_conv_kernel, tt=tt),
        out_shape=jax.ShapeDtypeStruct((n_tiles * tt, ch), BF16),
        grid=(n_tiles,),
        in_specs=[pl.BlockSpec((None, CONV_HALO, ch), lambda i: (i, 0, 0)),
                  pl.BlockSpec((tt, ch), lambda i: (blk0 + i, 0)),
                  pl.BlockSpec((CONV_WIDTH, ch), lambda i: (0, 0)), vec, vec, vec],
        out_specs=pl.BlockSpec((tt, ch), lambda i: (i, 0)),
        scratch_shapes=[pltpu.VMEM((CONV_HALO + tt, ch), F32),
                        pltpu.VMEM((7, tt + CONV_HALO - 8, ch), F32),
                        pltpu.VMEM((tt, ch), F32)],
        compiler_params=_params("parallel"), name="conv_module",
    )(halo, u_all, conv_w, conv_b, ln_g, ln_b)


def _xattn_kernel(q_ref, k_ref, v_ref, o_ref, *, heads, head_dim, scale):
    for h in range(heads):
        cols = slice(h * head_dim, (h + 1) * head_dim)
        s = _dot_nt(q_ref[:, cols], k_ref[:, cols].astype(BF16)) * scale
        e = jnp.exp(s - jnp.max(s, axis=-1, keepdims=True))
        p = e / jnp.sum(e, axis=-1, keepdims=True)
        o_ref[:, cols] = _dot(p.astype(BF16), v_ref[:, cols].astype(BF16)).astype(o_ref.dtype)


def _xattn(qx, mem_k, mem_v, *, row0, batch, seq, tt, heads, head_dim):
    width = heads * head_dim
    n_mem = mem_k.shape[0] // batch
    nt = seq // tt
    blk0 = row0 // tt
    mem_spec = pl.BlockSpec((n_mem, width), lambda b, i: (b, 0))
    return pl.pallas_call(
        functools.partial(_xattn_kernel, heads=heads, head_dim=head_dim, scale=head_dim ** -0.5),
        out_shape=jax.ShapeDtypeStruct((batch * seq, width), BF16),
        grid=(batch, nt),
        in_specs=[pl.BlockSpec((tt, width), lambda b, i: (blk0 + b * nt + i, 0)), mem_spec, mem_spec],
        out_specs=pl.BlockSpec((tt, width), lambda b, i: (b * nt + i, 0)),
        compiler_params=_params("parallel", "parallel"), name="memory_cross_attention",
    )(qx, mem_k, mem_v)


def _split_bf16(x):
    hi = x.astype(BF16)
    return hi, (x - hi.astype(F32)).astype(BF16)


def _router_kernel(x_ref, w_ref, o_ref):
    xh, xl = _split_bf16(x_ref[...])
    wh, wl = _split_bf16(w_ref[...])
    o_ref[...] = _dot(xh, wh) + (_dot(xh, wl) + _dot(xl, wh))


def _router_logits(x, w, *, tm):
    m, d = x.shape
    e = w.shape[1]
    return pl.pallas_call(
        _router_kernel, out_shape=jax.ShapeDtypeStruct((m, e), F32), grid=(m // tm,),
        in_specs=[pl.BlockSpec((tm, d), lambda i: (i, 0)), pl.BlockSpec((d, e), lambda i: (0, 0))],
        out_specs=pl.BlockSpec((tm, e), lambda i: (i, 0)),
        compiler_params=_params("parallel"), name="router_logits",
    )(x, w)


def _route(logits, b_router):
    t = logits.shape[0]
    scores = jax.nn.sigmoid(logits)
    choice = scores + b_router.astype(F32)
    grp = choice.reshape(t, N_GROUPS, N_EXPERTS // N_GROUPS)
    grp_score = jnp.sum(lax.top_k(grp, 2)[0], axis=-1)
    top_g = lax.top_k(grp_score, TOPK_GROUPS)[1]
    g_keep = jnp.any(top_g[:, :, None] == jnp.arange(N_GROUPS)[None, None, :], axis=1)
    e_keep = jnp.repeat(g_keep, N_EXPERTS // N_GROUPS, axis=1)
    idx = lax.top_k(jnp.where(e_keep, choice, -jnp.inf), TOP_K)[1]
    w = jnp.take_along_axis(scores, idx, axis=1)
    w = w / jnp.sum(w, axis=-1, keepdims=True) * ROUTED_SCALE
    return idx, w


def _expert_tiles(idx, bm):
    t, k = idx.shape
    n_assign = t * k
    n_tiles = n_assign // bm + N_EXPERTS
    flat_e = idx.reshape(-1).astype(jnp.int32)
    order = jnp.argsort(flat_e).astype(jnp.int32)
    tok_sorted = order // k
    counts = jnp.bincount(flat_e, length=N_EXPERTS).astype(jnp.int32)
    starts = jnp.cumsum(counts) - counts
    tiles_e = (counts + bm - 1) // bm
    tile_end = jnp.cumsum(tiles_e)
    tile_first = tile_end - tiles_e
    n_valid = tile_end[-1]
    tid = jnp.arange(n_tiles, dtype=jnp.int32)
    valid = tid < n_valid
    tile_ref = jnp.where(valid, tid, n_valid - 1)
    tile_e = jnp.minimum(jnp.searchsorted(tile_end, tile_ref, side='right'), N_EXPERTS - 1).astype(jnp.int32)
    r = tile_ref - tile_first[tile_e]
    tile_src = starts[tile_e] + r * bm
    tile_n = jnp.where(valid, jnp.clip(counts[tile_e] - r * bm, 0, bm), 0)
    inv = jnp.zeros((n_assign,), jnp.int32).at[order].set(jnp.arange(n_assign, dtype=jnp.int32))
    rank = inv - starts[flat_e]
    pos = (tile_first[flat_e] + rank // bm) * bm + rank % bm
    return dict(n_tiles=n_tiles, tok_sorted=tok_sorted, tile_e=tile_e, tile_blk=tile_ref.astype(jnp.int32),
                tile_src=tile_src.astype(jnp.int32), tile_n=tile_n.astype(jnp.int32),
                tile_valid=valid.astype(jnp.int32), pos=pos.astype(jnp.int32))


def _row_gather_copy(x_hbm, buf, sem, src_row, slot, dst_row):
    return pltpu.make_async_copy(x_hbm.at[pl.ds(src_row, 1), :], buf.at[slot, pl.ds(dst_row, 1), :], sem.at[slot])


def _expert_up_kernel(tile_e, tile_blk, tile_src, tile_n, tile_valid, tok_sorted,
                      x_hbm, wg_ref, wu_ref, h_ref, xbuf, xb, sem):
    t = pl.program_id(0)
    f = pl.program_id(1)
    slot = t % 2

    def start_gather(tile, dst_slot):
        src = tile_src[tile]

        def body(i, c):
            _row_gather_copy(x_hbm, xbuf, sem, tok_sorted[src + i], dst_slot, i).start()
            return c

        lax.fori_loop(0, tile_n[tile], body, 0)

    @pl.when(jnp.logical_and(t == 0, f == 0))
    def _():
        xbuf[...] = jnp.zeros(xbuf.shape, xbuf.dtype)
        start_gather(0, 0)

    @pl.when(f == 0)
    def _():
        @pl.when(t + 1 < pl.num_programs(0))
        def _():
            start_gather(t + 1, 1 - slot)

        def wait_body(i, c):
            _row_gather_copy(x_hbm, xbuf, sem, 0, slot, i).wait()
            return c

        lax.fori_loop(0, tile_n[t], wait_body, 0)

        @pl.when(tile_n[t] > 0)
        def _():
            xb[...] = xbuf[slot].astype(BF16)

    @pl.when(tile_n[t] > 0)
    def _():
        x = xb[...]
        gate = _dot(x, wg_ref[...].astype(BF16))
        up = _dot(x, wu_ref[...].astype(BF16))
        h_ref[...] = (jax.nn.silu(gate) * up).astype(h_ref.dtype)


def _expert_up(x2d, w_gate, w_up, plan, *, bm, fc):
    d = x2d.shape[1]
    f_total = w_gate.shape[2]
    nf = f_total // fc
    n_tiles = plan["n_tiles"]

    def w_map(t, f, te, tb, ts, tn, tv, tok):
        return (te[t], 0, jnp.where(tv[t] > 0, f, nf - 1))

    def h_map(t, f, te, tb, ts, tn, tv, tok):
        return (tb[t], jnp.where(tv[t] > 0, f, nf - 1))

    grid_spec = pltpu.PrefetchScalarGridSpec(
        num_scalar_prefetch=6, grid=(n_tiles, nf),
        in_specs=[pl.BlockSpec(memory_space=pl.ANY),
                  pl.BlockSpec((None, d, fc), w_map), pl.BlockSpec((None, d, fc), w_map)],
        out_specs=pl.BlockSpec((bm, fc), h_map),
        scratch_shapes=[pltpu.VMEM((2, bm, d), F32), pltpu.VMEM((bm, d), BF16), pltpu.SemaphoreType.DMA((2,))])
    return pl.pallas_call(
        _expert_up_kernel, grid_spec=grid_spec,
        out_shape=jax.ShapeDtypeStruct((n_tiles * bm, f_total), BF16),
        compiler_params=_params("arbitrary", "arbitrary"), name="routed_experts_up",
    )(plan["tile_e"], plan["tile_blk"], plan["tile_src"], plan["tile_n"], plan["tile_valid"],
      plan["tok_sorted"], x2d, w_gate, w_up)


def _expert_down_kernel(tile_e, tile_blk, tile_n, tile_valid, h_ref, wd_ref, y_ref):
    @pl.when(tile_n[pl.program_id(0)] > 0)
    def _():
        y_ref[...] = _dot(h_ref[...], wd_ref[...].astype(BF16))


def _expert_down(h, w_down, plan, *, bm, nc):
    f_total = h.shape[1]
    d = w_down.shape[2]
    n_cols = d // nc
    n_tiles = plan["n_tiles"]

    def col(c, tv, t):
        return jnp.where(tv[t] > 0, c, n_cols - 1)

    grid_spec = pltpu.PrefetchScalarGridSpec(
        num_scalar_prefetch=4, grid=(n_tiles, n_cols),
        in_specs=[pl.BlockSpec((bm, f_total), lambda t, c, te, tb, tn, tv: (tb[t], 0)),
                  pl.BlockSpec((None, f_total, nc), lambda t, c, te, tb, tn, tv: (te[t], 0, col(c, tv, t)))],
        out_specs=pl.BlockSpec((bm, nc), lambda t, c, te, tb, tn, tv: (tb[t], col(c, tv, t))))
    return pl.pallas_call(
        _expert_down_kernel, grid_spec=grid_spec,
        out_shape=jax.ShapeDtypeStruct((n_tiles * bm, d), F32),
        compiler_params=_params("arbitrary", "arbitrary"), name="routed_experts_down",
    )(plan["tile_e"], plan["tile_blk"], plan["tile_n"], plan["tile_valid"], h, w_down)


COMBINE_ROWS = 8


def _combine_kernel(pos, y_hbm, w_ref, o_ref, buf, sem, *, tt, top_k):
    i = pl.program_id(0)
    slot = i % 2

    def start_gather(tile, dst_slot):
        def body(j, c):
            base = (tile * tt + j) * top_k
            for k in range(top_k):
                pltpu.make_async_copy(y_hbm.at[pl.ds(pos[base + k], 1), :],
                                      buf.at[dst_slot, pl.ds(k * tt + j, 1), :], sem.at[dst_slot]).start()
            return c

        lax.fori_loop(0, tt, body, 0)

    @pl.when(i == 0)
    def _():
        start_gather(0, 0)

    @pl.when(i + 1 < pl.num_programs(0))
    def _():
        start_gather(i + 1, 1 - slot)

    pltpu.make_async_copy(y_hbm.at[pl.ds(0, top_k * tt), :], buf.at[slot], sem.at[slot]).wait()

    def row_body(rc, c):
        r0 = pl.multiple_of(rc * COMBINE_ROWS, COMBINE_ROWS)
        w = w_ref[pl.ds(r0, COMBINE_ROWS), :]
        acc = jnp.zeros((COMBINE_ROWS, o_ref.shape[1]), F32)
        for k in range(top_k):
            acc = acc + w[:, k:k + 1] * buf[slot, pl.ds(k * tt + r0, COMBINE_ROWS), :]
        o_ref[pl.ds(r0, COMBINE_ROWS), :] = acc
        return c

    lax.fori_loop(0, tt // COMBINE_ROWS, row_body, 0)


def _combine(y, pos, w, *, tt):
    t, top_k = w.shape
    d = y.shape[1]
    grid_spec = pltpu.PrefetchScalarGridSpec(
        num_scalar_prefetch=1, grid=(t // tt,),
        in_specs=[pl.BlockSpec(memory_space=pl.ANY), pl.BlockSpec((tt, top_k), lambda i, p: (i, 0))],
        out_specs=pl.BlockSpec((tt, d), lambda i, p: (i, 0)),
        scratch_shapes=[pltpu.VMEM((2, top_k * tt, d), F32), pltpu.SemaphoreType.DMA((2,))])
    return pl.pallas_call(
        functools.partial(_combine_kernel, tt=tt, top_k=top_k), grid_spec=grid_spec,
        out_shape=jax.ShapeDtypeStruct((t, d), F32),
        compiler_params=_params("arbitrary"), name="routed_experts_combine",
    )(pos, y, w)


def _layer(x_prompt, x_sample, mem_prompt, cache_sb_k, cache_sb_v, state_conv, cache_mem_k, cache_mem_v,
           w_in, b_gate, conv_w, conv_b, conv_ln_g, conv_ln_b, w_br_a, w_br_b, w_out, ln1_g, ln1_b,
           xa_wq, xa_wk, xa_wv, xa_wo, ln2_g, ln2_b, w_router, b_router, w_exp_gate, w_exp_up, w_exp_down,
           w_sh_gate, w_sh_up, w_sh_down, ln3_g, ln3_b, *, alpha, tm, tn, sb_tq, sb_tk, conv_tt, xa_tt):
    bp, tp, d = x_prompt.shape
    bs, ts, _ = x_sample.shape
    mp, ms = bp * tp, bs * ts
    sb_w = SB_HEADS * SB_HEAD_DIM
    ch = conv_w.shape[1]
    xa_w = XA_HEADS * XA_HEAD_DIM
    n_mem = mem_prompt.shape[1]
    col_val, col_gate, col_ga, col_gb = 3 * sb_w, 3 * sb_w + ch, 3 * sb_w + 2 * ch, 3 * sb_w + 2 * ch + d
    row = lambda v: v.reshape(1, -1)

    x_all = jnp.concatenate([x_prompt.reshape(mp, d), x_sample.reshape(ms, d)], axis=0)
    xb = x_all.astype(BF16)
    w_in_b = w_in.astype(BF16)

    qb, k32, v32, kb, vb = _proj(xb, [(w_in_b, 0), (w_in_b, sb_w), (w_in_b, 2 * sb_w)], _ep_qkv,
                                 [BF16, F32, F32, BF16, BF16], n_out=sb_w, tm=tm, tn=tn, name="proj_qkv")
    (u32,) = _proj(xb, [(w_in_b, col_val), (w_in_b, col_gate)], _ep_glu, [F32],
                   n_out=ch, tm=tm, tn=tn, name="proj_glu")

    osb_p = _sb_prompt(qb, kb, vb, batch=bp, seq=tp, heads=SB_HEADS, head_dim=SB_HEAD_DIM, tq=sb_tq, tk=sb_tk)
    osb_s = _sb_sample(qb, kb, vb, cache_sb_k.reshape(bs, -1, sb_w), cache_sb_v.reshape(bs, -1, sb_w),
                       row0=mp, batch=bs, t_new=ts, heads=SB_HEADS, head_dim=SB_HEAD_DIM, tk=sb_tk)
    osb = jnp.concatenate([osb_p, osb_s], axis=0)

    u_p = u32[:mp].reshape(bp, tp // conv_tt, conv_tt, ch)
    halo_p = jnp.concatenate([jnp.zeros((bp, 1, CONV_HALO, ch), F32), u_p[:, :-1, conv_tt - CONV_HALO:, :]], axis=1)
    halo_s = jnp.concatenate([jnp.zeros((bs, CONV_HALO - CONV_STATE, ch), F32), state_conv], axis=1)
    conv_args = (conv_w, row(conv_b), row(conv_ln_g), row(conv_ln_b))
    c_p = _conv_module(u32, halo_p.reshape(-1, CONV_HALO, ch), *conv_args, row0=0, n_tiles=mp // conv_tt, tt=conv_tt)
    c_s = _conv_module(u32, halo_s, *conv_args, row0=mp, n_tiles=bs, tt=ts)
    cb = jnp.concatenate([c_p, c_s], axis=0)
    new_conv_p = u32[:mp].reshape(bp, tp, ch)[:, tp - CONV_STATE:, :]
    new_conv_s = jnp.concatenate([state_conv, u32[mp:].reshape(bs, ts, ch)], axis=1)[:, -CONV_STATE:, :]

    merged = _merge(xb, osb, cb, w_in_b, col_ga, col_gb, w_br_a.astype(BF16), w_br_b.astype(BF16),
                    row(b_gate[0]), row(b_gate[1]), tm=tm, tn=tn)
    h1, h1b = _mm_res_ln(merged, w_out.astype(BF16), x_all, row(ln1_g), row(ln1_b), alpha=alpha, tm=tm, tn=tn)

    (qx,) = _proj(h1b, [(xa_wq.astype(BF16), 0)], _ep_copy, [BF16], n_out=xa_w, tm=tm, tn=tn, name="proj_xattn_q")
    mk32, mv32, mkb, mvb = _proj(mem_prompt.reshape(bp * n_mem, d).astype(BF16),
                                 [(xa_wk.astype(BF16), 0), (xa_wv.astype(BF16), 0)], _ep_kv,
                                 [F32, F32, BF16, BF16], n_out=xa_w, tm=tm, tn=tn, name="proj_mem_kv")
    xo_p = _xattn(qx, mkb, mvb, row0=0, batch=bp, seq=tp, tt=xa_tt, heads=XA_HEADS, head_dim=XA_HEAD_DIM)
    xo_s = _xattn(qx, cache_mem_k.reshape(bs * n_mem, xa_w), cache_mem_v.reshape(bs * n_mem, xa_w),
                  row0=mp, batch=bs, seq=ts, tt=ts, heads=XA_HEADS, head_dim=XA_HEAD_DIM)
    xo = jnp.concatenate([xo_p, xo_s], axis=0)
    h2, h2b = _mm_res_ln(xo, xa_wo.astype(BF16), h1, row(ln2_g), row(ln2_b), alpha=alpha, tm=tm, tn=tn)

    logits = _router_logits(h2, w_router, tm=tm)
    idx, w_tok = _route(logits, b_router)
    plan = _expert_tiles(idx, EXPERT_TILE_ROWS)
    h_exp = _expert_up(h2, w_exp_gate, w_exp_up, plan, bm=EXPERT_TILE_ROWS, fc=256)
    y_exp = _expert_down(h_exp, w_exp_down, plan, bm=EXPERT_TILE_ROWS, nc=1024)
    routed = _combine(y_exp, plan["pos"], w_tok.astype(F32), tt=COMBINE_TILE_TOKENS)
    (sh,) = _proj(h2b, [(w_sh_gate.astype(BF16), 0), (w_sh_up.astype(BF16), 0)], _ep_swiglu, [BF16],
                  n_out=w_sh_gate.shape[1], tm=tm, tn=tn, name="proj_shared_up")
    y, _ = _mm_res_ln(sh, w_sh_down.astype(BF16), h2, row(ln3_g), row(ln3_b), alpha=alpha, tm=tm, tn=tn,
                      extra=routed)

    outs = dict(
        y_p=y[:mp].reshape(bp, tp, d), y_s=y[mp:].reshape(bs, ts, d),
        k_p=k32[:mp].reshape(bp, tp, SB_HEADS, SB_HEAD_DIM), v_p=v32[:mp].reshape(bp, tp, SB_HEADS, SB_HEAD_DIM),
        k_s=k32[mp:].reshape(bs, ts, SB_HEADS, SB_HEAD_DIM), v_s=v32[mp:].reshape(bs, ts, SB_HEADS, SB_HEAD_DIM),
        conv_p=new_conv_p, conv_s=new_conv_s,
        mk_p=mk32.reshape(bp, n_mem, XA_HEADS, XA_HEAD_DIM), mv_p=mv32.reshape(bp, n_mem, XA_HEADS, XA_HEAD_DIM))
    outs["dbg"] = dict(osb=osb, cb=cb, merged=merged, h1=h1, h2=h2, logits=logits, routed=routed, sh=sh,
                       xo=xo, u32=u32)
    return outs


def kernel(x_prompt, x_sample, mem_prompt, cache_sb_k, cache_sb_v, state_conv, cache_mem_k, cache_mem_v, w_in, b_gate, conv_w, conv_b, conv_ln_g, conv_ln_b, w_br_a, w_br_b, w_out, ln1_g, ln1_b, xa_wq, xa_wk, xa_wv, xa_wo, ln2_g, ln2_b, w_router, b_router, w_exp_gate, w_exp_up, w_exp_down, w_sh_gate, w_sh_up, w_sh_down, ln3_g, ln3_b):
    depth = w_in.shape[0]
    assert depth == 1, "one layer"
    alpha = (2.0 * depth) ** 0.25
    o = _layer(x_prompt, x_sample, mem_prompt, cache_sb_k[0], cache_sb_v[0], state_conv[0], cache_mem_k[0],
               cache_mem_v[0], w_in[0], b_gate[0], conv_w[0], conv_b[0], conv_ln_g[0], conv_ln_b[0], w_br_a[0],
               w_br_b[0], w_out[0], ln1_g[0], ln1_b[0], xa_wq[0], xa_wk[0], xa_wv[0], xa_wo[0], ln2_g[0], ln2_b[0],
               w_router[0], b_router[0], w_exp_gate[0], w_exp_up[0], w_exp_down[0], w_sh_gate[0], w_sh_up[0],
               w_sh_down[0], ln3_g[0], ln3_b[0],
               alpha=alpha, tm=512, tn=512, sb_tq=256, sb_tk=256, conv_tt=128, xa_tt=256)
    stack = lambda a: a[None]
    return (o["y_p"], o["y_s"], stack(o["k_p"]), stack(o["v_p"]), stack(o["conv_p"]), stack(o["mk_p"]),
            stack(o["mv_p"]), stack(o["k_s"]), stack(o["v_s"]), stack(o["conv_s"]))
```

```python
import functools

import jax
import jax.numpy as jnp
from jax import lax
from jax.experimental import pallas as pl
from jax.experimental.pallas import tpu as pltpu

F32 = jnp.float32
BF16 = jnp.bfloat16

V7X_VMEM_LIMIT_BYTES = 56 * 1024 * 1024
LANES = 128

SB_HEADS = 16
SB_HEAD_DIM = 128
XA_HEADS = 4
XA_HEAD_DIM = 256
CONV_WIDTH = 31
CONV_STATE = CONV_WIDTH - 1
CONV_HALO = 32
N_EXPERTS = 256
N_GROUPS = 8
TOPK_GROUPS = 4
TOP_K = 8
ROUTED_SCALE = 2.5
LN_EPS = 1e-5
EXPERT_TILE_ROWS = 512
EXPERT_ROW_GROUP = 128
COMBINE_TILE_TOKENS = 32
ROUTER_TILE_TOKENS = 256


def _params(*sem):
    return pltpu.CompilerParams(dimension_semantics=sem, vmem_limit_bytes=V7X_VMEM_LIMIT_BYTES)


def _dot(a, b):
    return jnp.dot(a, b, preferred_element_type=F32)


def _dot_nt(a, b):
    return lax.dot_general(a, b, (((1,), (1,)), ((), ())), preferred_element_type=F32)


def _proj_kernel(*refs, n_w, n_bias, epilogue):
    x_ref = refs[0]
    w_refs = refs[1:1 + n_w]
    b_refs = refs[1 + n_w:1 + n_w + n_bias]
    o_refs = refs[1 + n_w + n_bias:]
    x = x_ref[...]
    accs = [_dot(x, w[...]) for w in w_refs]
    outs = epilogue(accs, [b[...] for b in b_refs])
    for o_ref, o in zip(o_refs, outs):
        o_ref[...] = o.astype(o_ref.dtype)


def _proj(x, weights, epilogue, out_dtypes, *, n_out, tm, tn, biases=(), name):
    m, k = x.shape
    tn = min(tn, n_out)
    grid = (m // tm, n_out // tn)
    in_specs = [pl.BlockSpec((tm, k), lambda i, j: (i, 0))]
    operands = [x]
    for w, off in weights:
        in_specs.append(pl.BlockSpec((k, tn), functools.partial(lambda i, j, o: (0, o + j), o=off // tn)))
        operands.append(w)
    for b, off in biases:
        in_specs.append(pl.BlockSpec((1, tn), functools.partial(lambda i, j, o: (0, o + j), o=off // tn)))
        operands.append(b)
    out_shape = [jax.ShapeDtypeStruct((m, n_out), dt) for dt in out_dtypes]
    out_specs = [pl.BlockSpec((tm, tn), lambda i, j: (i, j)) for _ in out_dtypes]
    return pl.pallas_call(
        functools.partial(_proj_kernel, n_w=len(weights), n_bias=len(biases), epilogue=epilogue),
        out_shape=out_shape, grid=grid, in_specs=in_specs, out_specs=out_specs,
        compiler_params=_params("parallel", "parallel"), name=name)(*operands)


def _ep_qkv(accs, _):
    q, k, v = accs
    return q, k, v, k, v


def _ep_glu(accs, _):
    val, gate = accs
    return (val * jax.nn.sigmoid(gate),)


def _ep_swiglu(accs, _):
    gate, up = accs
    return (jax.nn.silu(gate) * up,)


def _ep_copy(accs, _):
    return tuple(accs)


def _ep_kv(accs, _):
    k, v = accs
    return k, v, k, v


def _merge_kernel(x_ref, osb_ref, c_ref, wga_ref, wgb_ref, wa_ref, wb_ref, ba_ref, bb_ref, o_ref):
    x = x_ref[...]
    gate_a = jax.nn.sigmoid(_dot(x, wga_ref[...]) + ba_ref[...])
    gate_b = jax.nn.sigmoid(_dot(x, wgb_ref[...]) + bb_ref[...])
    br_a = _dot(osb_ref[...], wa_ref[...])
    br_b = _dot(c_ref[...], wb_ref[...])
    o_ref[...] = (gate_a * br_a + gate_b * br_b).astype(o_ref.dtype)


def _merge(xb, osb, cb, w_in, ga_col, gb_col, w_a, w_b, bias_a, bias_b, *, tm, tn):
    m, d = xb.shape
    ka = osb.shape[1]
    kb = cb.shape[1]
    n = w_a.shape[1]
    return pl.pallas_call(
        _merge_kernel,
        out_shape=jax.ShapeDtypeStruct((m, n), BF16),
        grid=(m // tm, n // tn),
        in_specs=[
            pl.BlockSpec((tm, d), lambda i, j: (i, 0)),
            pl.BlockSpec((tm, ka), lambda i, j: (i, 0)),
            pl.BlockSpec((tm, kb), lambda i, j: (i, 0)),
            pl.BlockSpec((d, tn), lambda i, j: (0, ga_col // tn + j)),
            pl.BlockSpec((d, tn), lambda i, j: (0, gb_col // tn + j)),
            pl.BlockSpec((ka, tn), lambda i, j: (0, j)),
            pl.BlockSpec((kb, tn), lambda i, j: (0, j)),
            pl.BlockSpec((1, tn), lambda i, j: (0, j)),
            pl.BlockSpec((1, tn), lambda i, j: (0, j)),
        ],
        out_specs=pl.BlockSpec((tm, tn), lambda i, j: (i, j)),
        compiler_params=_params("parallel", "parallel"), name="merge_gated_branches",
    )(xb, osb, cb, w_in, w_in, w_a, w_b, bias_a, bias_b)


LN_ROWS = 16


def _layer_norm_rows(x, g, b):
    mu = jnp.mean(x, axis=-1, keepdims=True)
    xc = x - mu
    var = jnp.mean(xc * xc, axis=-1, keepdims=True)
    return xc * lax.rsqrt(var + LN_EPS) * g + b


def _mm_res_ln_kernel(*refs, tn, alpha, has_extra):
    if has_extra:
        a_ref, w_ref, res_ref, extra_ref, g_ref, b_ref, o32_ref, o16_ref = refs
    else:
        a_ref, w_ref, res_ref, g_ref, b_ref, o32_ref, o16_ref = refs
    j = pl.program_id(1)
    n_col_tiles = o32_ref.shape[1] // tn
    pre = alpha * res_ref[...] + _dot(a_ref[...], w_ref[...])
    if has_extra:
        pre = pre + extra_ref[...]
    for jj in range(n_col_tiles):
        @pl.when(j == jj)
        def _():
            o32_ref[:, jj * tn:(jj + 1) * tn] = pre

    @pl.when(j == n_col_tiles - 1)
    def _():
        g = g_ref[...]
        b = b_ref[...]

        def body(r, carry):
            rows = pl.ds(pl.multiple_of(r * LN_ROWS, LN_ROWS), LN_ROWS)
            y = _layer_norm_rows(o32_ref[rows, :], g, b)
            o32_ref[rows, :] = y
            o16_ref[rows, :] = y.astype(o16_ref.dtype)
            return carry

        lax.fori_loop(0, o32_ref.shape[0] // LN_ROWS, body, 0)


def _mm_res_ln(a, w, res, g, b, *, alpha, tm, tn, extra=None):
    m, k = a.shape
    n = w.shape[1]
    in_specs = [
        pl.BlockSpec((tm, k), lambda i, j: (i, 0)),
        pl.BlockSpec((k, tn), lambda i, j: (0, j)),
        pl.BlockSpec((tm, tn), lambda i, j: (i, j)),
    ]
    operands = [a, w, res]
    if extra is not None:
        in_specs.append(pl.BlockSpec((tm, tn), lambda i, j: (i, j)))
        operands.append(extra)
    in_specs += [pl.BlockSpec((1, n), lambda i, j: (0, 0)), pl.BlockSpec((1, n), lambda i, j: (0, 0))]
    operands += [g, b]
    return pl.pallas_call(
        functools.partial(_mm_res_ln_kernel, tn=tn, alpha=alpha, has_extra=extra is not None),
        out_shape=[jax.ShapeDtypeStruct((m, n), F32), jax.ShapeDtypeStruct((m, n), BF16)],
        grid=(m // tm, n // tn), in_specs=in_specs,
        out_specs=[pl.BlockSpec((tm, n), lambda i, j: (i, 0)), pl.BlockSpec((tm, n), lambda i, j: (i, 0))],
        compiler_params=_params("parallel", "arbitrary"), name="matmul_residual_layernorm",
    )(*operands)


def _sb_tile(q, k, v, tri, before, carry, scale):
    z = _dot_nt(q, k) * scale
    softplus = jnp.maximum(z, 0.0) + jnp.log(1.0 + jnp.exp(-jnp.abs(z)))
    log_stay = -softplus if before is None else jnp.where(before, -softplus, 0.0)
    hi = log_stay.astype(BF16)
    lo = (log_stay - hi.astype(F32)).astype(BF16)
    later = _dot(hi, tri) + _dot(lo, tri)
    arg = z + log_stay + later + carry
    if before is not None:
        arg = jnp.where(before, arg, -jnp.inf)
    att = jnp.exp(arg)
    out = _dot(att.astype(BF16), v)
    return out, carry + jnp.sum(log_stay, axis=1, keepdims=True)


def _later_key_matrix(tk):
    r = lax.broadcasted_iota(jnp.int32, (tk, tk), 0)
    c = lax.broadcasted_iota(jnp.int32, (tk, tk), 1)
    return (r > c).astype(BF16)


def _sb_prompt_kernel(q_ref, k_ref, v_ref, tri_ref, o_ref, *, tq, scale):
    qi = pl.program_id(2)
    q = q_ref[...]
    tri = tri_ref[...]
    r = lax.broadcasted_iota(jnp.int32, (tq, tq), 0)
    c = lax.broadcasted_iota(jnp.int32, (tq, tq), 1)
    diag = pl.ds(pl.multiple_of(qi * tq, tq), tq)
    out, carry = _sb_tile(q, k_ref[diag, :], v_ref[diag, :], tri, c < r, jnp.zeros((tq, 1), F32), scale)

    def body(it, state):
        out, carry = state
        rows = pl.ds(pl.multiple_of((qi - 1 - it) * tq, tq), tq)
        o_t, carry = _sb_tile(q, k_ref[rows, :], v_ref[rows, :], tri, None, carry, scale)
        return out + o_t, carry

    out, _ = lax.fori_loop(0, qi, body, (out, carry))
    o_ref[...] = out.astype(o_ref.dtype)


def _sb_prompt(qb, kb, vb, *, batch, seq, heads, head_dim, tq):
    nq = seq // tq
    return pl.pallas_call(
        functools.partial(_sb_prompt_kernel, tq=tq, scale=head_dim ** -0.5),
        out_shape=jax.ShapeDtypeStruct((batch * seq, heads * head_dim), BF16),
        grid=(batch, heads, nq),
        in_specs=[
            pl.BlockSpec((tq, head_dim), lambda b, h, i: (b * nq + i, h)),
            pl.BlockSpec((seq, head_dim), lambda b, h, i: (b, h)),
            pl.BlockSpec((seq, head_dim), lambda b, h, i: (b, h)),
            pl.BlockSpec((tq, tq), lambda b, h, i: (0, 0)),
        ],
        out_specs=pl.BlockSpec((tq, head_dim), lambda b, h, i: (b * nq + i, h)),
        compiler_params=_params("parallel", "parallel", "parallel"), name="stick_breaking_prompt",
    )(qb, kb, vb, _later_key_matrix(tq))


def _sb_sample_kernel(q_ref, kn_ref, vn_ref, kc_ref, vc_ref, tri_new_ref, tri_ref, o_ref, *,
                      heads, head_dim, tk, scale):
    t_new = q_ref.shape[0]
    past = kc_ref.shape[0]
    r = lax.broadcasted_iota(jnp.int32, (t_new, t_new), 0)
    c = lax.broadcasted_iota(jnp.int32, (t_new, t_new), 1)
    before_new = c < r
    tri_new = tri_new_ref[...]
    tri = tri_ref[...]

    for h in range(heads):
        cols = slice(h * head_dim, (h + 1) * head_dim)
        q = q_ref[:, cols]
        out, carry = _sb_tile(q, kn_ref[:, cols], vn_ref[:, cols], tri_new, before_new,
                              jnp.zeros((t_new, 1), F32), scale)
        for t in reversed(range(past // tk)):
            rows = slice(t * tk, (t + 1) * tk)
            o_t, carry = _sb_tile(q, kc_ref[rows, cols].astype(BF16), vc_ref[rows, cols].astype(BF16),
                                  tri, None, carry, scale)
            out = out + o_t
        o_ref[:, cols] = out.astype(o_ref.dtype)


def _sb_sample(qb, kb, vb, cache_k, cache_v, *, row0, batch, t_new, heads, head_dim, tk):
    width = heads * head_dim
    past = cache_k.shape[1]
    blk0 = row0 // t_new
    new_spec = pl.BlockSpec((t_new, width), lambda b: (blk0 + b, 0))
    cache_spec = pl.BlockSpec((None, past, width), lambda b: (b, 0, 0))
    return pl.pallas_call(
        functools.partial(_sb_sample_kernel, heads=heads, head_dim=head_dim, tk=tk, scale=head_dim ** -0.5),
        out_shape=jax.ShapeDtypeStruct((batch * t_new, width), BF16),
        grid=(batch,),
        in_specs=[new_spec, new_spec, new_spec, cache_spec, cache_spec,
                  pl.BlockSpec((t_new, t_new), lambda b: (0, 0)), pl.BlockSpec((tk, tk), lambda b: (0, 0))],
        out_specs=pl.BlockSpec((t_new, width), lambda b: (b, 0)),
        compiler_params=_params("parallel"), name="stick_breaking_sample",
    )(qb, kb, vb, cache_k, cache_v, _later_key_matrix(t_new), _later_key_matrix(tk))


CONV_ROWS = 16
CONV_LANES = 512


def _conv_kernel(halo_ref, u_ref, w_ref, cb_ref, g_ref, b_ref, o_ref, win_ref, shift_ref, pre_ref, *, tt):
    ch = u_ref.shape[1]
    win_ref[0:CONV_HALO, :] = halo_ref[...]
    win_ref[CONV_HALO:CONV_HALO + tt, :] = u_ref[...]
    n_shift_rows = shift_ref.shape[1]
    for s in range(1, 8):
        shift_ref[s - 1] = win_ref[s:s + n_shift_rows, :]

    def row_body(rc, carry):
        r0 = pl.multiple_of(rc * CONV_ROWS, CONV_ROWS)
        for lc in range(ch // CONV_LANES):
            lanes = slice(lc * CONV_LANES, (lc + 1) * CONV_LANES)
            acc = jnp.zeros((CONV_ROWS, CONV_LANES), F32)
            for j in range(CONV_WIDTH):
                off = j + CONV_HALO - CONV_STATE
                rows = pl.ds(r0 + (off // 8) * 8, CONV_ROWS)
                if off % 8 == 0:
                    x = win_ref[rows, lanes]
                else:
                    x = shift_ref[off % 8 - 1, rows, lanes]
                acc = acc + x * w_ref[j:j + 1, lanes]
            pre_ref[pl.ds(r0, CONV_ROWS), lanes] = acc + cb_ref[:, lanes]
        y = _layer_norm_rows(pre_ref[pl.ds(r0, CONV_ROWS), :], g_ref[...], b_ref[...])
        o_ref[pl.ds(r0, CONV_ROWS), :] = jax.nn.silu(y).astype(o_ref.dtype)
        return carry

    lax.fori_loop(0, tt // CONV_ROWS, row_body, 0)


def _conv_module(u_all, halo, conv_w, conv_b, ln_g, ln_b, *, row0, n_tiles, tt):
    ch = u_all.shape[1]
    blk0 = row0 // tt
    vec = pl.BlockSpec((1, ch), lambda i: (0, 0))
    return pl.pallas_call(
        functools.partial(_conv_kernel, tt=tt),
        out_shape=jax.ShapeDtypeStruct((n_tiles * tt, ch), BF16),
        grid=(n_tiles,),
        in_specs=[pl.BlockSpec((None, CONV_HALO, ch), lambda i: (i, 0, 0)),
                  pl.BlockSpec((tt, ch), lambda i: (blk0 + i, 0)),
                  pl.BlockSpec((CONV_WIDTH, ch), lambda i: (0, 0)), vec, vec, vec],
        out_specs=pl.BlockSpec((tt, ch), lambda i: (i, 0)),
        scratch_shapes=[pltpu.VMEM((CONV_HALO + tt, ch), F32),
                        pltpu.VMEM((7, tt + CONV_HALO - 8, ch), F32),
                        pltpu.VMEM((tt, ch), F32)],
        compiler_params=_params("parallel"), name="conv_module",
    )(halo, u_all, conv_w, conv_b, ln_g, ln_b)


def _xattn_kernel(q_ref, k_ref, v_ref, o_ref, *, heads, head_dim, scale):
    for h in range(heads):
        cols = slice(h * head_dim, (h + 1) * head_dim)
        s = _dot_nt(q_ref[:, cols], k_ref[:, cols].astype(BF16)) * scale
        e = jnp.exp(s - jnp.max(s, axis=-1, keepdims=True))
        p = e / jnp.sum(e, axis=-1, keepdims=True)
        o_ref[:, cols] = _dot(p.astype(BF16), v_ref[:, cols].astype(BF16)).astype(o_ref.dtype)


def _xattn(qx, mem_k, mem_v, *, row0, batch, seq, tt, heads, head_dim):
    width = heads * head_dim
    n_mem = mem_k.shape[0] // batch
    nt = seq // tt
    blk0 = row0 // tt
    mem_spec = pl.BlockSpec((n_mem, width), lambda b, i: (b, 0))
    return pl.pallas_call(
        functools.partial(_xattn_kernel, heads=heads, head_dim=head_dim, scale=head_dim ** -0.5),
        out_shape=jax.ShapeDtypeStruct((batch * seq, width), BF16),
        grid=(batch, nt),
        in_specs=[pl.BlockSpec((tt, width), lambda b, i: (blk0 + b * nt + i, 0)), mem_spec, mem_spec],
        out_specs=pl.BlockSpec((tt, width), lambda b, i: (b * nt + i, 0)),
        compiler_params=_params("parallel", "parallel"), name="memory_cross_attention",
    )(qx, mem_k, mem_v)


def _split_bf16(x):
    hi = x.astype(BF16)
    return hi, (x - hi.astype(F32)).astype(BF16)


def _first_argmax(x, lane_f):
    m = jnp.max(x, axis=-1, keepdims=True)
    return m, jnp.min(jnp.where(x == m, lane_f, float(x.shape[-1])), axis=-1, keepdims=True)


def _router_kernel(x_ref, w_ref, b_ref, tri_ref, idx_ref, wt_ref, rank_ref, cnt_ref, wh_ref, wl_ref, run_ref):
    @pl.when(pl.program_id(0) == 0)
    def _():
        wh, wl = _split_bf16(w_ref[...])
        wh_ref[...] = wh
        wl_ref[...] = wl
        run_ref[...] = jnp.zeros(run_ref.shape, run_ref.dtype)

    xh, xl = _split_bf16(x_ref[...])
    wh = wh_ref[...]
    logits = _dot(xh, wh) + (_dot(xh, wl_ref[...]) + _dot(xl, wh))
    scores = jax.nn.sigmoid(logits)
    choice = scores + b_ref[...]
    tm, n_e = choice.shape
    lane = lax.broadcasted_iota(jnp.int32, (tm, n_e), 1)
    lane_f = lane.astype(F32)
    group = lax.shift_right_logical(lane, (n_e // N_GROUPS).bit_length() - 1)
    neg = -jnp.inf

    gscore = jnp.zeros((tm, n_e), F32)
    per_group = []
    for g in range(N_GROUPS):
        in_g = group == g
        xg = jnp.where(in_g, choice, neg)
        m1, i1 = _first_argmax(xg, lane_f)
        m2 = jnp.max(jnp.where(lane_f == i1, neg, xg), axis=-1, keepdims=True)
        per_group.append(m1 + m2)
        gscore = jnp.where(in_g, m1 + m2, gscore)
    beaten = jnp.zeros((tm, n_e), F32)
    for g in range(N_GROUPS):
        s = per_group[g]
        wins = (s > gscore) | ((s == gscore) & (group > g))
        beaten = beaten + jnp.where(wins, 1.0, 0.0)
    cand = jnp.where(beaten < float(TOPK_GROUPS), choice, neg)

    col = lax.broadcasted_iota(jnp.int32, (tm, TOP_K), 1)
    idx_f = jnp.zeros((tm, TOP_K), F32)
    wraw = jnp.zeros((tm, TOP_K), F32)
    chosen = jnp.zeros((tm, n_e), F32)
    picks = []
    for k in range(TOP_K):
        _, ik = _first_argmax(cand, lane_f)
        hit = lane_f == ik
        sk = jnp.sum(jnp.where(hit, scores, 0.0), axis=-1, keepdims=True)
        idx_f = jnp.where(col == k, ik, idx_f)
        wraw = jnp.where(col == k, sk, wraw)
        cand = jnp.where(hit, neg, cand)
        chosen = jnp.where(hit, 1.0, chosen)
        picks.append(hit)
    idx_ref[...] = idx_f.astype(jnp.int32)
    wt_ref[...] = wraw / jnp.sum(wraw, axis=-1, keepdims=True) * ROUTED_SCALE

    earlier = _dot(tri_ref[...], chosen.astype(BF16)) + run_ref[...]
    rank_f = jnp.zeros((tm, TOP_K), F32)
    for k in range(TOP_K):
        rk = jnp.sum(jnp.where(picks[k], earlier, 0.0), axis=-1, keepdims=True)
        rank_f = jnp.where(col == k, rk, rank_f)
    rank_ref[...] = rank_f.astype(jnp.int32)
    run_ref[...] = run_ref[...] + jnp.sum(chosen, axis=0, keepdims=True)
    cnt_ref[...] = run_ref[...].astype(jnp.int32)


def _router(x, w, b, *, tm):
    m, d = x.shape
    e = w.shape[1]
    tok = lambda dt: jax.ShapeDtypeStruct((m, TOP_K), dt)
    tok_spec = pl.BlockSpec((tm, TOP_K), lambda i: (i, 0))
    return pl.pallas_call(
        _router_kernel,
        out_shape=[tok(jnp.int32), tok(F32), tok(jnp.int32), jax.ShapeDtypeStruct((1, e), jnp.int32)],
        grid=(m // tm,),
        in_specs=[pl.BlockSpec((tm, d), lambda i: (i, 0)), pl.BlockSpec((d, e), lambda i: (0, 0)),
                  pl.BlockSpec((1, e), lambda i: (0, 0)), pl.BlockSpec((tm, tm), lambda i: (0, 0))],
        out_specs=[tok_spec, tok_spec, tok_spec, pl.BlockSpec((1, e), lambda i: (0, 0))],
        scratch_shapes=[pltpu.VMEM((d, e), BF16), pltpu.VMEM((d, e), BF16), pltpu.VMEM((1, e), F32)],
        compiler_params=_params("arbitrary"), name="router_topk",
    )(x, w, b, _later_key_matrix(tm))


def _expert_tiles(idx, rank, counts, bm):
    t, k = idx.shape
    n_assign = t * k
    n_tiles = n_assign // bm + N_EXPERTS
    starts = jnp.cumsum(counts) - counts
    tiles_e = (counts + bm - 1) // bm
    tile_end = jnp.cumsum(tiles_e)
    tile_first = tile_end - tiles_e
    n_valid = tile_end[-1]
    tid = jnp.arange(n_tiles, dtype=jnp.int32)
    valid = tid < n_valid
    tile_ref = jnp.where(valid, tid, n_valid - 1)
    tile_e = jnp.minimum(jnp.sum(tile_end[None, :] <= tile_ref[:, None], axis=1), N_EXPERTS - 1).astype(jnp.int32)
    r = tile_ref - tile_first[tile_e]
    tile_src = starts[tile_e] + r * bm
    tile_n = jnp.where(valid, jnp.clip(counts[tile_e] - r * bm, 0, bm), 0)
    is_e = idx[:, :, None] == jnp.arange(N_EXPERTS, dtype=jnp.int32)
    first_of = jnp.sum(jnp.where(is_e, tile_first, 0), axis=-1)
    start_of = jnp.sum(jnp.where(is_e, starts, 0), axis=-1)
    pos = (first_of + rank // bm) * bm + rank % bm
    tok = jnp.broadcast_to(jnp.arange(t, dtype=jnp.int32)[:, None], (t, k))
    tok_sorted = jnp.zeros((n_assign,), jnp.int32).at[(start_of + rank).reshape(-1)].set(
        tok.reshape(-1), unique_indices=True)
    return dict(n_tiles=n_tiles, tok_sorted=tok_sorted, tile_e=tile_e, tile_blk=tile_ref.astype(jnp.int32),
                tile_src=tile_src.astype(jnp.int32), tile_n=tile_n.astype(jnp.int32),
                tile_valid=valid.astype(jnp.int32), pos=pos.reshape(-1).astype(jnp.int32))


def _row_gather_copy(x_hbm, buf, sem, src_row, slot, dst_row):
    return pltpu.make_async_copy(x_hbm.at[pl.ds(src_row, 1), :], buf.at[slot, pl.ds(dst_row, 1), :], sem.at[slot])


def _expert_up_kernel(tile_e, tile_blk, tile_src, tile_n, tile_valid, tok_sorted,
                      x_hbm, wg_ref, wu_ref, h_ref, xbuf, xb, wgb, wub, sem):
    t = pl.program_id(0)
    f = pl.program_id(1)
    slot = t % 2
    bm = xb.shape[0]
    share = bm // pl.num_programs(1)

    def start_gather(tile, dst_slot, lo, hi):
        src = tile_src[tile]

        def body(i, c):
            _row_gather_copy(x_hbm, xbuf, sem, tok_sorted[src + i], dst_slot, i).start()
            return c

        lax.fori_loop(lo, jnp.minimum(hi, tile_n[tile]), body, 0)

    @pl.when(jnp.logical_and(t == 0, f == 0))
    def _():
        xbuf[...] = jnp.zeros(xbuf.shape, xbuf.dtype)
        start_gather(0, 0, 0, bm)

    @pl.when(t + 1 < pl.num_programs(0))
    def _():
        start_gather(t + 1, 1 - slot, f * share, (f + 1) * share)

    n = tile_n[t]

    @pl.when(f == 0)
    def _():
        def wait_body(i, c):
            _row_gather_copy(x_hbm, xbuf, sem, 0, slot, i).wait()
            return c

        lax.fori_loop(0, n, wait_body, 0)

        @pl.when(n > 0)
        def _():
            xb[...] = xbuf[slot].astype(BF16)

    @pl.when(n > 0)
    def _():
        wgb[...] = wg_ref[...].astype(BF16)
        wub[...] = wu_ref[...].astype(BF16)
        groups = (n + EXPERT_ROW_GROUP - 1) // EXPERT_ROW_GROUP
        for g in range(1, bm // EXPERT_ROW_GROUP + 1):
            @pl.when(groups == g)
            def _():
                rows = slice(0, g * EXPERT_ROW_GROUP)
                x = xb[rows, :]
                h_ref[rows, :] = (jax.nn.silu(_dot(x, wgb[...])) * _dot(x, wub[...])).astype(h_ref.dtype)


def _expert_up(x2d, w_gate, w_up, plan, *, bm, fc):
    d = x2d.shape[1]
    f_total = w_gate.shape[2]
    nf = f_total // fc
    n_tiles = plan["n_tiles"]

    def w_map(t, f, te, tb, ts, tn, tv, tok):
        return (te[t], 0, jnp.where(tv[t] > 0, f, nf - 1))

    def h_map(t, f, te, tb, ts, tn, tv, tok):
        return (tb[t], jnp.where(tv[t] > 0, f, nf - 1))

    grid_spec = pltpu.PrefetchScalarGridSpec(
        num_scalar_prefetch=6, grid=(n_tiles, nf),
        in_specs=[pl.BlockSpec(memory_space=pl.ANY),
                  pl.BlockSpec((None, d, fc), w_map), pl.BlockSpec((None, d, fc), w_map)],
        out_specs=pl.BlockSpec((bm, fc), h_map),
        scratch_shapes=[pltpu.VMEM((2, bm, d), F32), pltpu.VMEM((bm, d), BF16), pltpu.VMEM((d, fc), BF16),
                        pltpu.VMEM((d, fc), BF16), pltpu.SemaphoreType.DMA((2,))])
    return pl.pallas_call(
        _expert_up_kernel, grid_spec=grid_spec,
        out_shape=jax.ShapeDtypeStruct((n_tiles * bm, f_total), BF16),
        compiler_params=_params("arbitrary", "arbitrary"), name="routed_experts_up",
    )(plan["tile_e"], plan["tile_blk"], plan["tile_src"], plan["tile_n"], plan["tile_valid"],
      plan["tok_sorted"], x2d, w_gate, w_up)


def _expert_down_kernel(tile_e, tile_blk, tile_n, tile_valid, h_ref, wd_ref, y_ref, wdb):
    n = tile_n[pl.program_id(0)]

    @pl.when(n > 0)
    def _():
        wdb[...] = wd_ref[...].astype(BF16)
        groups = (n + EXPERT_ROW_GROUP - 1) // EXPERT_ROW_GROUP
        for g in range(1, h_ref.shape[0] // EXPERT_ROW_GROUP + 1):
            @pl.when(groups == g)
            def _():
                rows = slice(0, g * EXPERT_ROW_GROUP)
                y_ref[rows, :] = _dot(h_ref[rows, :], wdb[...])


def _expert_down(h, w_down, plan, *, bm, nc):
    f_total = h.shape[1]
    d = w_down.shape[2]
    n_cols = d // nc
    n_tiles = plan["n_tiles"]

    def col(c, tv, t):
        return jnp.where(tv[t] > 0, c, n_cols - 1)

    grid_spec = pltpu.PrefetchScalarGridSpec(
        num_scalar_prefetch=4, grid=(n_tiles, n_cols),
        in_specs=[pl.BlockSpec((bm, f_total), lambda t, c, te, tb, tn, tv: (tb[t], 0)),
                  pl.BlockSpec((None, f_total, nc), lambda t, c, te, tb, tn, tv: (te[t], 0, col(c, tv, t)))],
        out_specs=pl.BlockSpec((bm, nc), lambda t, c, te, tb, tn, tv: (tb[t], col(c, tv, t))),
        scratch_shapes=[pltpu.VMEM((f_total, nc), BF16)])
    return pl.pallas_call(
        _expert_down_kernel, grid_spec=grid_spec,
        out_shape=jax.ShapeDtypeStruct((n_tiles * bm, d), F32),
        compiler_params=_params("arbitrary", "arbitrary"), name="routed_experts_down",
    )(plan["tile_e"], plan["tile_blk"], plan["tile_n"], plan["tile_valid"], h, w_down)


COMBINE_ROWS = 8


def _combine_kernel(pos, y_hbm, w_ref, o_ref, buf, sem, *, tt, top_k):
    i = pl.program_id(0)
    slot = i % 2

    def start_gather(tile, dst_slot):
        def body(j, c):
            base = (tile * tt + j) * top_k
            for k in range(top_k):
                pltpu.make_async_copy(y_hbm.at[pl.ds(pos[base + k], 1), :],
                                      buf.at[dst_slot, pl.ds(k * tt + j, 1), :], sem.at[dst_slot]).start()
            return c

        lax.fori_loop(0, tt, body, 0)

    @pl.when(i == 0)
    def _():
        start_gather(0, 0)

    @pl.when(i + 1 < pl.num_programs(0))
    def _():
        start_gather(i + 1, 1 - slot)

    pltpu.make_async_copy(y_hbm.at[pl.ds(0, top_k * tt), :], buf.at[slot], sem.at[slot]).wait()

    def row_body(rc, c):
        r0 = pl.multiple_of(rc * COMBINE_ROWS, COMBINE_ROWS)
        w = w_ref[pl.ds(r0, COMBINE_ROWS), :]
        acc = jnp.zeros((COMBINE_ROWS, o_ref.shape[1]), F32)
        for k in range(top_k):
            acc = acc + w[:, k:k + 1] * buf[slot, pl.ds(k * tt + r0, COMBINE_ROWS), :]
        o_ref[pl.ds(r0, COMBINE_ROWS), :] = acc
        return c

    lax.fori_loop(0, tt // COMBINE_ROWS, row_body, 0)


def _combine(y, pos, w, *, tt):
    t, top_k = w.shape
    d = y.shape[1]
    grid_spec = pltpu.PrefetchScalarGridSpec(
        num_scalar_prefetch=1, grid=(t // tt,),
        in_specs=[pl.BlockSpec(memory_space=pl.ANY), pl.BlockSpec((tt, top_k), lambda i, p: (i, 0))],
        out_specs=pl.BlockSpec((tt, d), lambda i, p: (i, 0)),
        scratch_shapes=[pltpu.VMEM((2, top_k * tt, d), F32), pltpu.SemaphoreType.DMA((2,))])
    return pl.pallas_call(
        functools.partial(_combine_kernel, tt=tt, top_k=top_k), grid_spec=grid_spec,
        out_shape=jax.ShapeDtypeStruct((t, d), F32),
        compiler_params=_params("arbitrary"), name="routed_experts_combine",
    )(pos, y, w)


def _layer(x_prompt, x_sample, mem_prompt, cache_sb_k, cache_sb_v, state_conv, cache_mem_k, cache_mem_v,
           w_in, b_gate, conv_w, conv_b, conv_ln_g, conv_ln_b, w_br_a, w_br_b, w_out, ln1_g, ln1_b,
           xa_wq, xa_wk, xa_wv, xa_wo, ln2_g, ln2_b, w_router, b_router, w_exp_gate, w_exp_up, w_exp_down,
           w_sh_gate, w_sh_up, w_sh_down, ln3_g, ln3_b, *, alpha, tm, tn, sb_tq, sb_tk, conv_tt, xa_tt):
    bp, tp, d = x_prompt.shape
    bs, ts, _ = x_sample.shape
    mp, ms = bp * tp, bs * ts
    sb_w = SB_HEADS * SB_HEAD_DIM
    ch = conv_w.shape[1]
    xa_w = XA_HEADS * XA_HEAD_DIM
    n_mem = mem_prompt.shape[1]
    col_val, col_gate, col_ga, col_gb = 3 * sb_w, 3 * sb_w + ch, 3 * sb_w + 2 * ch, 3 * sb_w + 2 * ch + d
    row = lambda v: v.reshape(1, -1)

    x_all = jnp.concatenate([x_prompt.reshape(mp, d), x_sample.reshape(ms, d)], axis=0)
    xb = x_all.astype(BF16)
    w_in_b = w_in.astype(BF16)

    qb, k32, v32, kb, vb = _proj(xb, [(w_in_b, 0), (w_in_b, sb_w), (w_in_b, 2 * sb_w)], _ep_qkv,
                                 [BF16, F32, F32, BF16, BF16], n_out=sb_w, tm=tm, tn=tn, name="proj_qkv")
    (u32,) = _proj(xb, [(w_in_b, col_val), (w_in_b, col_gate)], _ep_glu, [F32],
                   n_out=ch, tm=tm, tn=tn, name="proj_glu")

    osb_p = _sb_prompt(qb, kb, vb, batch=bp, seq=tp, heads=SB_HEADS, head_dim=SB_HEAD_DIM, tq=sb_tq)
    osb_s = _sb_sample(qb, kb, vb, cache_sb_k.reshape(bs, -1, sb_w), cache_sb_v.reshape(bs, -1, sb_w),
                       row0=mp, batch=bs, t_new=ts, heads=SB_HEADS, head_dim=SB_HEAD_DIM, tk=sb_tk)
    osb = jnp.concatenate([osb_p, osb_s], axis=0)

    u_p = u32[:mp].reshape(bp, tp // conv_tt, conv_tt, ch)
    halo_p = jnp.concatenate([jnp.zeros((bp, 1, CONV_HALO, ch), F32), u_p[:, :-1, conv_tt - CONV_HALO:, :]], axis=1)
    halo_s = jnp.concatenate([jnp.zeros((bs, CONV_HALO - CONV_STATE, ch), F32), state_conv], axis=1)
    conv_args = (conv_w, row(conv_b), row(conv_ln_g), row(conv_ln_b))
    c_p = _conv_module(u32, halo_p.reshape(-1, CONV_HALO, ch), *conv_args, row0=0, n_tiles=mp // conv_tt, tt=conv_tt)
    c_s = _conv_module(u32, halo_s, *conv_args, row0=mp, n_tiles=bs, tt=ts)
    cb = jnp.concatenate([c_p, c_s], axis=0)
    new_conv_p = u32[:mp].reshape(bp, tp, ch)[:, tp - CONV_STATE:, :]
    new_conv_s = jnp.concatenate([state_conv, u32[mp:].reshape(bs, ts, ch)], axis=1)[:, -CONV_STATE:, :]

    merged = _merge(xb, osb, cb, w_in_b, col_ga, col_gb, w_br_a.astype(BF16), w_br_b.astype(BF16),
                    row(b_gate[0]), row(b_gate[1]), tm=tm, tn=tn)
    h1, h1b = _mm_res_ln(merged, w_out.astype(BF16), x_all, row(ln1_g), row(ln1_b), alpha=alpha, tm=tm, tn=tn)

    (qx,) = _proj(h1b, [(xa_wq.astype(BF16), 0)], _ep_copy, [BF16], n_out=xa_w, tm=tm, tn=tn, name="proj_xattn_q")
    mk32, mv32, mkb, mvb = _proj(mem_prompt.reshape(bp * n_mem, d).astype(BF16),
                                 [(xa_wk.astype(BF16), 0), (xa_wv.astype(BF16), 0)], _ep_kv,
                                 [F32, F32, BF16, BF16], n_out=xa_w, tm=tm, tn=tn, name="proj_mem_kv")
    xo_p = _xattn(qx, mkb, mvb, row0=0, batch=bp, seq=tp, tt=xa_tt, heads=XA_HEADS, head_dim=XA_HEAD_DIM)
    xo_s = _xattn(qx, cache_mem_k.reshape(bs * n_mem, xa_w), cache_mem_v.reshape(bs * n_mem, xa_w),
                  row0=mp, batch=bs, seq=ts, tt=ts, heads=XA_HEADS, head_dim=XA_HEAD_DIM)
    xo = jnp.concatenate([xo_p, xo_s], axis=0)
    h2, h2b = _mm_res_ln(xo, xa_wo.astype(BF16), h1, row(ln2_g), row(ln2_b), alpha=alpha, tm=tm, tn=tn)

    idx, w_tok, rank, counts = _router(h2, w_router, row(b_router), tm=ROUTER_TILE_TOKENS)
    plan = _expert_tiles(idx, rank, counts[0], EXPERT_TILE_ROWS)
    h_exp = _expert_up(h2, w_exp_gate, w_exp_up, plan, bm=EXPERT_TILE_ROWS, fc=256)
    y_exp = _expert_down(h_exp, w_exp_down, plan, bm=EXPERT_TILE_ROWS, nc=1024)
    routed = _combine(y_exp, plan["pos"], w_tok, tt=COMBINE_TILE_TOKENS)
    (sh,) = _proj(h2b, [(w_sh_gate.astype(BF16), 0), (w_sh_up.astype(BF16), 0)], _ep_swiglu, [BF16],
                  n_out=w_sh_gate.shape[1], tm=tm, tn=tn, name="proj_shared_up")
    y, _ = _mm_res_ln(sh, w_sh_down.astype(BF16), h2, row(ln3_g), row(ln3_b), alpha=alpha, tm=tm, tn=tn,
                      extra=routed)

    outs = dict(
        y_p=y[:mp].reshape(bp, tp, d), y_s=y[mp:].reshape(bs, ts, d),
        k_p=k32[:mp].reshape(bp, tp, SB_HEADS, SB_HEAD_DIM), v_p=v32[:mp].reshape(bp, tp, SB_HEADS, SB_HEAD_DIM),
        k_s=k32[mp:].reshape(bs, ts, SB_HEADS, SB_HEAD_DIM), v_s=v32[mp:].reshape(bs, ts, SB_HEADS, SB_HEAD_DIM),
        conv_p=new_conv_p, conv_s=new_conv_s,
        mk_p=mk32.reshape(bp, n_mem, XA_HEADS, XA_HEAD_DIM), mv_p=mv32.reshape(bp, n_mem, XA_HEADS, XA_HEAD_DIM))
    return outs


def kernel(x_prompt, x_sample, mem_prompt, cache_sb_k, cache_sb_v, state_conv, cache_mem_k, cache_mem_v, w_in, b_gate, conv_w, conv_b, conv_ln_g, conv_ln_b, w_br_a, w_br_b, w_out, ln1_g, ln1_b, xa_wq, xa_wk, xa_wv, xa_wo, ln2_g, ln2_b, w_router, b_router, w_exp_gate, w_exp_up, w_exp_down, w_sh_gate, w_sh_up, w_sh_down, ln3_g, ln3_b):
    depth = w_in.shape[0]
    assert depth == 1, "one layer"
    alpha = (2.0 * depth) ** 0.25
    o = _layer(x_prompt, x_sample, mem_prompt, cache_sb_k[0], cache_sb_v[0], state_conv[0], cache_mem_k[0],
               cache_mem_v[0], w_in[0], b_gate[0], conv_w[0], conv_b[0], conv_ln_g[0], conv_ln_b[0], w_br_a[0],
               w_br_b[0], w_out[0], ln1_g[0], ln1_b[0], xa_wq[0], xa_wk[0], xa_wv[0], xa_wo[0], ln2_g[0], ln2_b[0],
               w_router[0], b_router[0], w_exp_gate[0], w_exp_up[0], w_exp_down[0], w_sh_gate[0], w_sh_up[0],
               w_sh_down[0], ln3_g[0], ln3_b[0],
               alpha=alpha, tm=512, tn=512, sb_tq=256, sb_tk=256, conv_tt=128, xa_tt=256)
    stack = lambda a: a[None]
    return (o["y_p"], o["y_s"], stack(o["k_p"]), stack(o["v_p"]), stack(o["conv_p"]), stack(o["mk_p"]),
            stack(o["mv_p"]), stack(o["k_s"]), stack(o["v_s"]), stack(o["conv_s"]))
```

```python
import functools

import jax
import jax.numpy as jnp
from jax import lax
from jax.experimental import pallas as pl
from jax.experimental.pallas import tpu as pltpu

F32 = jnp.float32
BF16 = jnp.bfloat16

V7X_VMEM_LIMIT_BYTES = 56 * 1024 * 1024
LANES = 128

SB_HEADS = 16
SB_HEAD_DIM = 128
XA_HEADS = 4
XA_HEAD_DIM = 256
CONV_WIDTH = 31
CONV_STATE = CONV_WIDTH - 1
CONV_HALO = 32
N_EXPERTS = 256
N_GROUPS = 8
TOPK_GROUPS = 4
TOP_K = 8
ROUTED_SCALE = 2.5
LN_EPS = 1e-5
EXPERT_TILE_ROWS = 384
EXPERT_ROW_GROUP = 128
COMBINE_TILE_TOKENS = 64
ROUTER_TILE_TOKENS = 256


def _params(*sem):
    return pltpu.CompilerParams(dimension_semantics=sem, vmem_limit_bytes=V7X_VMEM_LIMIT_BYTES)


def _dot(a, b):
    return jnp.dot(a, b, preferred_element_type=F32)


def _dot_nt(a, b):
    return lax.dot_general(a, b, (((1,), (1,)), ((), ())), preferred_element_type=F32)


def _proj_kernel(*refs, n_w, n_bias, epilogue):
    x_ref = refs[0]
    w_refs = refs[1:1 + n_w]
    b_refs = refs[1 + n_w:1 + n_w + n_bias]
    o_refs = refs[1 + n_w + n_bias:]
    x = x_ref[...]
    accs = [_dot(x, w[...]) for w in w_refs]
    outs = epilogue(accs, [b[...] for b in b_refs])
    for o_ref, o in zip(o_refs, outs):
        o_ref[...] = o.astype(o_ref.dtype)


def _proj(x, weights, epilogue, out_dtypes, *, n_out, tm, tn, biases=(), name):
    m, k = x.shape
    tn = min(tn, n_out)
    grid = (m // tm, n_out // tn)
    in_specs = [pl.BlockSpec((tm, k), lambda i, j: (i, 0))]
    operands = [x]
    for w, off in weights:
        in_specs.append(pl.BlockSpec((k, tn), functools.partial(lambda i, j, o: (0, o + j), o=off // tn)))
        operands.append(w)
    for b, off in biases:
        in_specs.append(pl.BlockSpec((1, tn), functools.partial(lambda i, j, o: (0, o + j), o=off // tn)))
        operands.append(b)
    out_shape = [jax.ShapeDtypeStruct((m, n_out), dt) for dt in out_dtypes]
    out_specs = [pl.BlockSpec((tm, tn), lambda i, j: (i, j)) for _ in out_dtypes]
    return pl.pallas_call(
        functools.partial(_proj_kernel, n_w=len(weights), n_bias=len(biases), epilogue=epilogue),
        out_shape=out_shape, grid=grid, in_specs=in_specs, out_specs=out_specs,
        compiler_params=_params("parallel", "parallel"), name=name)(*operands)


def _ep_qkv(accs, _):
    q, k, v = accs
    return q, k, v, k, v


def _ep_glu(accs, _):
    val, gate = accs
    return (val * jax.nn.sigmoid(gate),)


def _ep_swiglu(accs, _):
    gate, up = accs
    return (jax.nn.silu(gate) * up,)


def _ep_copy(accs, _):
    return tuple(accs)


def _ep_kv(accs, _):
    k, v = accs
    return k, v, k, v


def _merge_kernel(x_ref, osb_ref, c_ref, wga_ref, wgb_ref, wa_ref, wb_ref, ba_ref, bb_ref, o_ref):
    x = x_ref[...]
    gate_a = jax.nn.sigmoid(_dot(x, wga_ref[...]) + ba_ref[...])
    gate_b = jax.nn.sigmoid(_dot(x, wgb_ref[...]) + bb_ref[...])
    br_a = _dot(osb_ref[...], wa_ref[...])
    br_b = _dot(c_ref[...], wb_ref[...])
    o_ref[...] = (gate_a * br_a + gate_b * br_b).astype(o_ref.dtype)


def _merge(xb, osb, cb, w_in, ga_col, gb_col, w_a, w_b, bias_a, bias_b, *, tm, tn):
    m, d = xb.shape
    ka = osb.shape[1]
    kb = cb.shape[1]
    n = w_a.shape[1]
    return pl.pallas_call(
        _merge_kernel,
        out_shape=jax.ShapeDtypeStruct((m, n), BF16),
        grid=(m // tm, n // tn),
        in_specs=[
            pl.BlockSpec((tm, d), lambda i, j: (i, 0)),
            pl.BlockSpec((tm, ka), lambda i, j: (i, 0)),
            pl.BlockSpec((tm, kb), lambda i, j: (i, 0)),
            pl.BlockSpec((d, tn), lambda i, j: (0, ga_col // tn + j)),
            pl.BlockSpec((d, tn), lambda i, j: (0, gb_col // tn + j)),
            pl.BlockSpec((ka, tn), lambda i, j: (0, j)),
            pl.BlockSpec((kb, tn), lambda i, j: (0, j)),
            pl.BlockSpec((1, tn), lambda i, j: (0, j)),
            pl.BlockSpec((1, tn), lambda i, j: (0, j)),
        ],
        out_specs=pl.BlockSpec((tm, tn), lambda i, j: (i, j)),
        compiler_params=_params("parallel", "parallel"), name="merge_gated_branches",
    )(xb, osb, cb, w_in, w_in, w_a, w_b, bias_a, bias_b)


LN_ROWS = 16


def _layer_norm_rows(x, g, b):
    mu = jnp.mean(x, axis=-1, keepdims=True)
    xc = x - mu
    var = jnp.mean(xc * xc, axis=-1, keepdims=True)
    return xc * lax.rsqrt(var + LN_EPS) * g + b


def _mm_res_ln_kernel(*refs, tn, alpha, has_extra):
    if has_extra:
        a_ref, w_ref, res_ref, extra_ref, g_ref, b_ref, o32_ref, o16_ref = refs
    else:
        a_ref, w_ref, res_ref, g_ref, b_ref, o32_ref, o16_ref = refs
    j = pl.program_id(1)
    n_col_tiles = o32_ref.shape[1] // tn
    pre = alpha * res_ref[...] + _dot(a_ref[...], w_ref[...])
    if has_extra:
        pre = pre + extra_ref[...]
    for jj in range(n_col_tiles):
        @pl.when(j == jj)
        def _():
            o32_ref[:, jj * tn:(jj + 1) * tn] = pre

    @pl.when(j == n_col_tiles - 1)
    def _():
        g = g_ref[...]
        b = b_ref[...]

        def body(r, carry):
            rows = pl.ds(pl.multiple_of(r * LN_ROWS, LN_ROWS), LN_ROWS)
            y = _layer_norm_rows(o32_ref[rows, :], g, b)
            o32_ref[rows, :] = y
            o16_ref[rows, :] = y.astype(o16_ref.dtype)
            return carry

        lax.fori_loop(0, o32_ref.shape[0] // LN_ROWS, body, 0)


def _mm_res_ln(a, w, res, g, b, *, alpha, tm, tn, extra=None):
    m, k = a.shape
    n = w.shape[1]
    in_specs = [
        pl.BlockSpec((tm, k), lambda i, j: (i, 0)),
        pl.BlockSpec((k, tn), lambda i, j: (0, j)),
        pl.BlockSpec((tm, tn), lambda i, j: (i, j)),
    ]
    operands = [a, w, res]
    if extra is not None:
        in_specs.append(pl.BlockSpec((tm, tn), lambda i, j: (i, j)))
        operands.append(extra)
    in_specs += [pl.BlockSpec((1, n), lambda i, j: (0, 0)), pl.BlockSpec((1, n), lambda i, j: (0, 0))]
    operands += [g, b]
    return pl.pallas_call(
        functools.partial(_mm_res_ln_kernel, tn=tn, alpha=alpha, has_extra=extra is not None),
        out_shape=[jax.ShapeDtypeStruct((m, n), F32), jax.ShapeDtypeStruct((m, n), BF16)],
        grid=(m // tm, n // tn), in_specs=in_specs,
        out_specs=[pl.BlockSpec((tm, n), lambda i, j: (i, 0)), pl.BlockSpec((tm, n), lambda i, j: (i, 0))],
        compiler_params=_params("parallel", "arbitrary"), name="matmul_residual_layernorm",
    )(*operands)


def _sb_tile(q, k, v, tri, before, carry, scale):
    z = _dot_nt(q, k) * scale
    softplus = jnp.maximum(z, 0.0) + jnp.log(1.0 + jnp.exp(-jnp.abs(z)))
    log_stay = -softplus if before is None else jnp.where(before, -softplus, 0.0)
    hi = log_stay.astype(BF16)
    lo = (log_stay - hi.astype(F32)).astype(BF16)
    later = _dot(hi, tri) + _dot(lo, tri)
    arg = z + log_stay + later + carry
    if before is not None:
        arg = jnp.where(before, arg, -jnp.inf)
    att = jnp.exp(arg)
    out = _dot(att.astype(BF16), v)
    return out, carry + jnp.sum(log_stay, axis=1, keepdims=True)


def _later_key_matrix(tk):
    r = lax.broadcasted_iota(jnp.int32, (tk, tk), 0)
    c = lax.broadcasted_iota(jnp.int32, (tk, tk), 1)
    return (r > c).astype(BF16)


def _sb_prompt_kernel(q_ref, k_ref, v_ref, tri_ref, o_ref, *, tq, scale):
    qi = pl.program_id(2)
    q = q_ref[...]
    tri = tri_ref[...]
    r = lax.broadcasted_iota(jnp.int32, (tq, tq), 0)
    c = lax.broadcasted_iota(jnp.int32, (tq, tq), 1)
    diag = pl.ds(pl.multiple_of(qi * tq, tq), tq)
    out, carry = _sb_tile(q, k_ref[diag, :], v_ref[diag, :], tri, c < r, jnp.zeros((tq, 1), F32), scale)

    def body(it, state):
        out, carry = state
        rows = pl.ds(pl.multiple_of((qi - 1 - it) * tq, tq), tq)
        o_t, carry = _sb_tile(q, k_ref[rows, :], v_ref[rows, :], tri, None, carry, scale)
        return out + o_t, carry

    out, _ = lax.fori_loop(0, qi, body, (out, carry))
    o_ref[...] = out.astype(o_ref.dtype)


def _sb_prompt(qb, kb, vb, *, batch, seq, heads, head_dim, tq):
    nq = seq // tq
    return pl.pallas_call(
        functools.partial(_sb_prompt_kernel, tq=tq, scale=head_dim ** -0.5),
        out_shape=jax.ShapeDtypeStruct((batch * seq, heads * head_dim), BF16),
        grid=(batch, heads, nq),
        in_specs=[
            pl.BlockSpec((tq, head_dim), lambda b, h, i: (b * nq + i, h)),
            pl.BlockSpec((seq, head_dim), lambda b, h, i: (b, h)),
            pl.BlockSpec((seq, head_dim), lambda b, h, i: (b, h)),
            pl.BlockSpec((tq, tq), lambda b, h, i: (0, 0)),
        ],
        out_specs=pl.BlockSpec((tq, head_dim), lambda b, h, i: (b * nq + i, h)),
        compiler_params=_params("parallel", "parallel", "parallel"), name="stick_breaking_prompt",
    )(qb, kb, vb, _later_key_matrix(tq))


def _sb_sample_kernel(q_ref, kn_ref, vn_ref, kc_ref, vc_ref, tri_new_ref, tri_ref, o_ref, *,
                      heads, head_dim, tk, scale):
    t_new = q_ref.shape[0]
    past = kc_ref.shape[0] // heads
    r = lax.broadcasted_iota(jnp.int32, (t_new, t_new), 0)
    c = lax.broadcasted_iota(jnp.int32, (t_new, t_new), 1)
    before_new = c < r
    tri_new = tri_new_ref[...]
    tri = tri_ref[...]

    for h in range(heads):
        cols = slice(h * head_dim, (h + 1) * head_dim)
        q = q_ref[:, cols]
        out, carry = _sb_tile(q, kn_ref[:, cols], vn_ref[:, cols], tri_new, before_new,
                              jnp.zeros((t_new, 1), F32), scale)
        for t in reversed(range(past // tk)):
            rows = pl.ds(t * tk * heads + h, tk, stride=heads)
            o_t, carry = _sb_tile(q, kc_ref[rows, :].astype(BF16), vc_ref[rows, :].astype(BF16),
                                  tri, None, carry, scale)
            out = out + o_t
        o_ref[:, cols] = out.astype(o_ref.dtype)


def _sb_sample(qb, kb, vb, cache_k, cache_v, *, row0, batch, t_new, heads, head_dim, tk):
    width = heads * head_dim
    blk0 = row0 // t_new
    new_spec = pl.BlockSpec((t_new, width), lambda b: (blk0 + b, 0))
    cache_spec = pl.BlockSpec((None, cache_k.shape[1], head_dim), lambda b: (b, 0, 0))
    return pl.pallas_call(
        functools.partial(_sb_sample_kernel, heads=heads, head_dim=head_dim, tk=tk, scale=head_dim ** -0.5),
        out_shape=jax.ShapeDtypeStruct((batch * t_new, width), BF16),
        grid=(batch,),
        in_specs=[new_spec, new_spec, new_spec, cache_spec, cache_spec,
                  pl.BlockSpec((t_new, t_new), lambda b: (0, 0)), pl.BlockSpec((tk, tk), lambda b: (0, 0))],
        out_specs=pl.BlockSpec((t_new, width), lambda b: (b, 0)),
        compiler_params=_params("parallel"), name="stick_breaking_sample",
    )(qb, kb, vb, cache_k, cache_v, _later_key_matrix(t_new), _later_key_matrix(tk))


CONV_ROWS = 16
CONV_LANES = 512


def _conv_kernel(halo_ref, u_ref, w_ref, cb_ref, g_ref, b_ref, o_ref, win_ref, shift_ref, pre_ref, *, tt):
    ch = u_ref.shape[1]
    win_ref[0:CONV_HALO, :] = halo_ref[...]
    win_ref[CONV_HALO:CONV_HALO + tt, :] = u_ref[...]
    n_shift_rows = shift_ref.shape[1]
    for s in range(1, 8):
        shift_ref[s - 1] = win_ref[s:s + n_shift_rows, :]

    def row_body(rc, carry):
        r0 = pl.multiple_of(rc * CONV_ROWS, CONV_ROWS)
        for lc in range(ch // CONV_LANES):
            lanes = slice(lc * CONV_LANES, (lc + 1) * CONV_LANES)
            acc = jnp.zeros((CONV_ROWS, CONV_LANES), F32)
            for j in range(CONV_WIDTH):
                off = j + CONV_HALO - CONV_STATE
                rows = pl.ds(r0 + (off // 8) * 8, CONV_ROWS)
                if off % 8 == 0:
                    x = win_ref[rows, lanes]
                else:
                    x = shift_ref[off % 8 - 1, rows, lanes]
                acc = acc + x * w_ref[j:j + 1, lanes]
            pre_ref[pl.ds(r0, CONV_ROWS), lanes] = acc + cb_ref[:, lanes]
        y = _layer_norm_rows(pre_ref[pl.ds(r0, CONV_ROWS), :], g_ref[...], b_ref[...])
        o_ref[pl.ds(r0, CONV_ROWS), :] = jax.nn.silu(y).astype(o_ref.dtype)
        return carry

    lax.fori_loop(0, tt // CONV_ROWS, row_body, 0)


def _conv_module(u_all, halo, conv_w, conv_b, ln_g, ln_b, *, row0, n_tiles, tt):
    ch = u_all.shape[1]
    blk0 = row0 // tt
    vec = pl.BlockSpec((1, ch), lambda i: (0, 0))
    return pl.pallas_call(
        functools.partial(_conv_kernel, tt=tt),
        out_shape=jax.ShapeDtypeStruct((n_tiles * tt, ch), BF16),
        grid=(n_tiles,),
        in_specs=[pl.BlockSpec((None, CONV_HALO, ch), lambda i: (i, 0, 0)),
                  pl.BlockSpec((tt, ch), lambda i: (blk0 + i, 0)),
                  pl.BlockSpec((CONV_WIDTH, ch), lambda i: (0, 0)), vec, vec, vec],
        out_specs=pl.BlockSpec((tt, ch), lambda i: (i, 0)),
        scratch_shapes=[pltpu.VMEM((CONV_HALO + tt, ch), F32),
                        pltpu.VMEM((7, tt + CONV_HALO - 8, ch), F32),
                        pltpu.VMEM((tt, ch), F32)],
        compiler_params=_params("parallel"), name="conv_module",
    )(halo, u_all, conv_w, conv_b, ln_g, ln_b)


def _xattn_kernel(q_ref, k_ref, v_ref, o_ref, *, heads, head_dim, scale):
    for h in range(heads):
        cols = slice(h * head_dim, (h + 1) * head_dim)
        s = _dot_nt(q_ref[:, cols], k_ref[:, cols].astype(BF16)) * scale
        e = jnp.exp(s - jnp.max(s, axis=-1, keepdims=True))
        p = e / jnp.sum(e, axis=-1, keepdims=True)
        o_ref[:, cols] = _dot(p.astype(BF16), v_ref[:, cols].astype(BF16)).astype(o_ref.dtype)


def _xattn(qx, mem_k, mem_v, *, row0, batch, seq, tt, heads, head_dim):
    width = heads * head_dim
    n_mem = mem_k.shape[0] // batch
    nt = seq // tt
    blk0 = row0 // tt
    mem_spec = pl.BlockSpec((n_mem, width), lambda b, i: (b, 0))
    return pl.pallas_call(
        functools.partial(_xattn_kernel, heads=heads, head_dim=head_dim, scale=head_dim ** -0.5),
        out_shape=jax.ShapeDtypeStruct((batch * seq, width), BF16),
        grid=(batch, nt),
        in_specs=[pl.BlockSpec((tt, width), lambda b, i: (blk0 + b * nt + i, 0)), mem_spec, mem_spec],
        out_specs=pl.BlockSpec((tt, width), lambda b, i: (b * nt + i, 0)),
        compiler_params=_params("parallel", "parallel"), name="memory_cross_attention",
    )(qx, mem_k, mem_v)


def _split_bf16(x):
    hi = x.astype(BF16)
    return hi, (x - hi.astype(F32)).astype(BF16)


def _first_argmax(x, lane_f):
    m = jnp.max(x, axis=-1, keepdims=True)
    return m, jnp.min(jnp.where(x == m, lane_f, float(x.shape[-1])), axis=-1, keepdims=True)


def _router_kernel(x_ref, w_ref, b_ref, tri_ref, idx_ref, wt_ref, rank_ref, cnt_ref, wh_ref, wl_ref, run_ref):
    @pl.when(pl.program_id(0) == 0)
    def _():
        wh, wl = _split_bf16(w_ref[...])
        wh_ref[...] = wh
        wl_ref[...] = wl
        run_ref[...] = jnp.zeros(run_ref.shape, run_ref.dtype)

    xh, xl = _split_bf16(x_ref[...])
    wh = wh_ref[...]
    logits = _dot(xh, wh) + (_dot(xh, wl_ref[...]) + _dot(xl, wh))
    scores = jax.nn.sigmoid(logits)
    choice = scores + b_ref[...]
    tm, n_e = choice.shape
    lane = lax.broadcasted_iota(jnp.int32, (tm, n_e), 1)
    lane_f = lane.astype(F32)
    group = lax.shift_right_logical(lane, (n_e // N_GROUPS).bit_length() - 1)
    neg = -jnp.inf

    gscore = jnp.zeros((tm, n_e), F32)
    per_group = []
    for g in range(N_GROUPS):
        in_g = group == g
        xg = jnp.where(in_g, choice, neg)
        m1, i1 = _first_argmax(xg, lane_f)
        m2 = jnp.max(jnp.where(lane_f == i1, neg, xg), axis=-1, keepdims=True)
        per_group.append(m1 + m2)
        gscore = jnp.where(in_g, m1 + m2, gscore)
    beaten = jnp.zeros((tm, n_e), F32)
    for g in range(N_GROUPS):
        s = per_group[g]
        wins = (s > gscore) | ((s == gscore) & (group > g))
        beaten = beaten + jnp.where(wins, 1.0, 0.0)
    cand = jnp.where(beaten < float(TOPK_GROUPS), choice, neg)

    col = lax.broadcasted_iota(jnp.int32, (tm, TOP_K), 1)
    idx_f = jnp.zeros((tm, TOP_K), F32)
    wraw = jnp.zeros((tm, TOP_K), F32)
    chosen = jnp.zeros((tm, n_e), F32)
    picks = []
    for k in range(TOP_K):
        _, ik = _first_argmax(cand, lane_f)
        hit = lane_f == ik
        sk = jnp.sum(jnp.where(hit, scores, 0.0), axis=-1, keepdims=True)
        idx_f = jnp.where(col == k, ik, idx_f)
        wraw = jnp.where(col == k, sk, wraw)
        cand = jnp.where(hit, neg, cand)
        chosen = jnp.where(hit, 1.0, chosen)
        picks.append(hit)
    idx_ref[...] = idx_f.astype(jnp.int32)
    wt_ref[...] = wraw / jnp.sum(wraw, axis=-1, keepdims=True) * ROUTED_SCALE

    earlier = _dot(tri_ref[...], chosen.astype(BF16)) + run_ref[...]
    rank_f = jnp.zeros((tm, TOP_K), F32)
    for k in range(TOP_K):
        rk = jnp.sum(jnp.where(picks[k], earlier, 0.0), axis=-1, keepdims=True)
        rank_f = jnp.where(col == k, rk, rank_f)
    rank_ref[...] = rank_f.astype(jnp.int32)
    run_ref[...] = run_ref[...] + jnp.sum(chosen, axis=0, keepdims=True)
    cnt_ref[...] = run_ref[...].astype(jnp.int32)


def _router(x, w, b, *, tm):
    m, d = x.shape
    e = w.shape[1]
    tok = lambda dt: jax.ShapeDtypeStruct((m, TOP_K), dt)
    tok_spec = pl.BlockSpec((tm, TOP_K), lambda i: (i, 0))
    return pl.pallas_call(
        _router_kernel,
        out_shape=[tok(jnp.int32), tok(F32), tok(jnp.int32), jax.ShapeDtypeStruct((1, e), jnp.int32)],
        grid=(m // tm,),
        in_specs=[pl.BlockSpec((tm, d), lambda i: (i, 0)), pl.BlockSpec((d, e), lambda i: (0, 0)),
                  pl.BlockSpec((1, e), lambda i: (0, 0)), pl.BlockSpec((tm, tm), lambda i: (0, 0))],
        out_specs=[tok_spec, tok_spec, tok_spec, pl.BlockSpec((1, e), lambda i: (0, 0))],
        scratch_shapes=[pltpu.VMEM((d, e), BF16), pltpu.VMEM((d, e), BF16), pltpu.VMEM((1, e), F32)],
        compiler_params=_params("arbitrary"), name="router_topk",
    )(x, w, b, _later_key_matrix(tm))


def _expert_tiles(idx, rank, counts, bm):
    t, k = idx.shape
    n_assign = t * k
    n_tiles = n_assign // bm + N_EXPERTS
    starts = jnp.cumsum(counts) - counts
    tiles_e = (counts + bm - 1) // bm
    tile_end = jnp.cumsum(tiles_e)
    tile_first = tile_end - tiles_e
    n_valid = tile_end[-1]
    tid = jnp.arange(n_tiles, dtype=jnp.int32)
    valid = tid < n_valid
    tile_ref = jnp.where(valid, tid, n_valid - 1)
    tile_e = jnp.minimum(jnp.sum(tile_end[None, :] <= tile_ref[:, None], axis=1), N_EXPERTS - 1).astype(jnp.int32)
    r = tile_ref - tile_first[tile_e]
    tile_src = starts[tile_e] + r * bm
    tile_n = jnp.where(valid, jnp.clip(counts[tile_e] - r * bm, 0, bm), 0)
    shift = n_assign.bit_length()
    packed = jnp.sum(jnp.where(idx[:, :, None] == jnp.arange(N_EXPERTS, dtype=jnp.int32),
                               tile_first * (1 << shift) + starts, 0), axis=-1)
    first_of = packed >> shift
    start_of = packed & ((1 << shift) - 1)
    pos = (first_of + rank // bm) * bm + rank % bm
    tok = jnp.broadcast_to(jnp.arange(t, dtype=jnp.int32)[:, None], (t, k))
    tok_sorted = jnp.zeros((n_assign,), jnp.int32).at[(start_of + rank).reshape(-1)].set(
        tok.reshape(-1), unique_indices=True)
    return dict(n_tiles=n_tiles, tok_sorted=tok_sorted, tile_e=tile_e, tile_blk=tile_ref.astype(jnp.int32),
                tile_src=tile_src.astype(jnp.int32), tile_n=tile_n.astype(jnp.int32),
                tile_valid=valid.astype(jnp.int32), pos=pos.reshape(-1).astype(jnp.int32))


def _row_gather_copy(x_hbm, buf, sem, src_row, slot, dst_row):
    return pltpu.make_async_copy(x_hbm.at[pl.ds(src_row, 1), :], buf.at[slot, pl.ds(dst_row, 1), :], sem.at[slot])


def _expert_up_kernel(tile_e, tile_blk, tile_src, tile_n, tile_valid, tok_sorted,
                      x_hbm, wg_ref, wu_ref, h_ref, xbuf, xb, wgb, wub, sem):
    t = pl.program_id(0)
    f = pl.program_id(1)
    slot = t % 2
    bm = xb.shape[0]
    share = bm // pl.num_programs(1)

    def start_gather(tile, dst_slot, lo, hi):
        src = tile_src[tile]

        def body(i, c):
            _row_gather_copy(x_hbm, xbuf, sem, tok_sorted[src + i], dst_slot, i).start()
            return c

        lax.fori_loop(lo, jnp.minimum(hi, tile_n[tile]), body, 0)

    @pl.when(jnp.logical_and(t == 0, f == 0))
    def _():
        xbuf[...] = jnp.zeros(xbuf.shape, xbuf.dtype)
        start_gather(0, 0, 0, bm)

    @pl.when(t + 1 < pl.num_programs(0))
    def _():
        start_gather(t + 1, 1 - slot, f * share, (f + 1) * share)

    n = tile_n[t]

    @pl.when(f == 0)
    def _():
        def wait_body(i, c):
            _row_gather_copy(x_hbm, xbuf, sem, 0, slot, i).wait()
            return c

        lax.fori_loop(0, n, wait_body, 0)

        @pl.when(n > 0)
        def _():
            xb[...] = xbuf[slot].astype(BF16)

    @pl.when(n > 0)
    def _():
        wgb[...] = wg_ref[...].astype(BF16)
        wub[...] = wu_ref[...].astype(BF16)
        groups = (n + EXPERT_ROW_GROUP - 1) // EXPERT_ROW_GROUP
        for g in range(1, bm // EXPERT_ROW_GROUP + 1):
            @pl.when(groups == g)
            def _():
                rows = slice(0, g * EXPERT_ROW_GROUP)
                x = xb[rows, :]
                h_ref[rows, :] = (jax.nn.silu(_dot(x, wgb[...])) * _dot(x, wub[...])).astype(h_ref.dtype)


def _expert_up(x2d, w_gate, w_up, plan, *, bm, fc):
    d = x2d.shape[1]
    f_total = w_gate.shape[2]
    nf = f_total // fc
    n_tiles = plan["n_tiles"]

    def w_map(t, f, te, tb, ts, tn, tv, tok):
        return (te[t], 0, jnp.where(tv[t] > 0, f, nf - 1))

    def h_map(t, f, te, tb, ts, tn, tv, tok):
        return (tb[t], jnp.where(tv[t] > 0, f, nf - 1))

    grid_spec = pltpu.PrefetchScalarGridSpec(
        num_scalar_prefetch=6, grid=(n_tiles, nf),
        in_specs=[pl.BlockSpec(memory_space=pl.ANY),
                  pl.BlockSpec((None, d, fc), w_map), pl.BlockSpec((None, d, fc), w_map)],
        out_specs=pl.BlockSpec((bm, fc), h_map),
        scratch_shapes=[pltpu.VMEM((2, bm, d), F32), pltpu.VMEM((bm, d), BF16), pltpu.VMEM((d, fc), BF16),
                        pltpu.VMEM((d, fc), BF16), pltpu.SemaphoreType.DMA((2,))])
    return pl.pallas_call(
        _expert_up_kernel, grid_spec=grid_spec,
        out_shape=jax.ShapeDtypeStruct((n_tiles * bm, f_total), BF16),
        compiler_params=_params("arbitrary", "arbitrary"), name="routed_experts_up",
    )(plan["tile_e"], plan["tile_blk"], plan["tile_src"], plan["tile_n"], plan["tile_valid"],
      plan["tok_sorted"], x2d, w_gate, w_up)


def _expert_down_kernel(tile_e, tile_blk, tile_n, tile_valid, h_ref, wd_ref, y_ref, wdb):
    n = tile_n[pl.program_id(0)]

    @pl.when(n > 0)
    def _():
        wdb[...] = wd_ref[...].astype(BF16)
        groups = (n + EXPERT_ROW_GROUP - 1) // EXPERT_ROW_GROUP
        for g in range(1, h_ref.shape[0] // EXPERT_ROW_GROUP + 1):
            @pl.when(groups == g)
            def _():
                rows = slice(0, g * EXPERT_ROW_GROUP)
                y_ref[rows, :] = _dot(h_ref[rows, :], wdb[...])


def _expert_down(h, w_down, plan, *, bm, nc):
    f_total = h.shape[1]
    d = w_down.shape[2]
    n_cols = d // nc
    n_tiles = plan["n_tiles"]

    def col(c, tv, t):
        return jnp.where(tv[t] > 0, c, n_cols - 1)

    grid_spec = pltpu.PrefetchScalarGridSpec(
        num_scalar_prefetch=4, grid=(n_tiles, n_cols),
        in_specs=[pl.BlockSpec((bm, f_total), lambda t, c, te, tb, tn, tv: (tb[t], 0)),
                  pl.BlockSpec((None, f_total, nc), lambda t, c, te, tb, tn, tv: (te[t], 0, col(c, tv, t)))],
        out_specs=pl.BlockSpec((bm, nc), lambda t, c, te, tb, tn, tv: (tb[t], col(c, tv, t))),
        scratch_shapes=[pltpu.VMEM((f_total, nc), BF16)])
    return pl.pallas_call(
        _expert_down_kernel, grid_spec=grid_spec,
        out_shape=jax.ShapeDtypeStruct((n_tiles * bm, d), F32),
        compiler_params=_params("arbitrary", "arbitrary"), name="routed_experts_down",
    )(plan["tile_e"], plan["tile_blk"], plan["tile_n"], plan["tile_valid"], h, w_down)


COMBINE_ROWS = 8


def _combine_kernel(pos, y_hbm, w_ref, o_ref, buf, sem, *, tt, top_k):
    i = pl.program_id(0)
    slot = i % 2

    def start_gather(tile, dst_slot):
        def body(j, c):
            base = (tile * tt + j) * top_k
            for k in range(top_k):
                pltpu.make_async_copy(y_hbm.at[pl.ds(pos[base + k], 1), :],
                                      buf.at[dst_slot, pl.ds(k * tt + j, 1), :], sem.at[dst_slot]).start()
            return c

        lax.fori_loop(0, tt, body, 0)

    @pl.when(i == 0)
    def _():
        start_gather(0, 0)

    @pl.when(i + 1 < pl.num_programs(0))
    def _():
        start_gather(i + 1, 1 - slot)

    pltpu.make_async_copy(y_hbm.at[pl.ds(0, top_k * tt), :], buf.at[slot], sem.at[slot]).wait()

    def row_body(rc, c):
        r0 = pl.multiple_of(rc * COMBINE_ROWS, COMBINE_ROWS)
        w = w_ref[pl.ds(r0, COMBINE_ROWS), :]
        acc = jnp.zeros((COMBINE_ROWS, o_ref.shape[1]), F32)
        for k in range(top_k):
            acc = acc + w[:, k:k + 1] * buf[slot, pl.ds(k * tt + r0, COMBINE_ROWS), :]
        o_ref[pl.ds(r0, COMBINE_ROWS), :] = acc
        return c

    lax.fori_loop(0, tt // COMBINE_ROWS, row_body, 0)


def _combine(y, pos, w, *, tt):
    t, top_k = w.shape
    d = y.shape[1]
    grid_spec = pltpu.PrefetchScalarGridSpec(
        num_scalar_prefetch=1, grid=(t // tt,),
        in_specs=[pl.BlockSpec(memory_space=pl.ANY), pl.BlockSpec((tt, top_k), lambda i, p: (i, 0))],
        out_specs=pl.BlockSpec((tt, d), lambda i, p: (i, 0)),
        scratch_shapes=[pltpu.VMEM((2, top_k * tt, d), F32), pltpu.SemaphoreType.DMA((2,))])
    return pl.pallas_call(
        functools.partial(_combine_kernel, tt=tt, top_k=top_k), grid_spec=grid_spec,
        out_shape=jax.ShapeDtypeStruct((t, d), F32),
        compiler_params=_params("arbitrary"), name="routed_experts_combine",
    )(pos, y, w)


def _layer(x_prompt, x_sample, mem_prompt, cache_sb_k, cache_sb_v, state_conv, cache_mem_k, cache_mem_v,
           w_in, b_gate, conv_w, conv_b, conv_ln_g, conv_ln_b, w_br_a, w_br_b, w_out, ln1_g, ln1_b,
           xa_wq, xa_wk, xa_wv, xa_wo, ln2_g, ln2_b, w_router, b_router, w_exp_gate, w_exp_up, w_exp_down,
           w_sh_gate, w_sh_up, w_sh_down, ln3_g, ln3_b, *, alpha, tm, tn, sb_tq, sb_tk, conv_tt, xa_tt):
    bp, tp, d = x_prompt.shape
    bs, ts, _ = x_sample.shape
    mp, ms = bp * tp, bs * ts
    sb_w = SB_HEADS * SB_HEAD_DIM
    ch = conv_w.shape[1]
    xa_w = XA_HEADS * XA_HEAD_DIM
    n_mem = mem_prompt.shape[1]
    col_val, col_gate, col_ga, col_gb = 3 * sb_w, 3 * sb_w + ch, 3 * sb_w + 2 * ch, 3 * sb_w + 2 * ch + d
    row = lambda v: v.reshape(1, -1)

    x_all = jnp.concatenate([x_prompt.reshape(mp, d), x_sample.reshape(ms, d)], axis=0)
    xb = x_all.astype(BF16)
    w_in_b = w_in.astype(BF16)

    qb, k32, v32, kb, vb = _proj(xb, [(w_in_b, 0), (w_in_b, sb_w), (w_in_b, 2 * sb_w)], _ep_qkv,
                                 [BF16, F32, F32, BF16, BF16], n_out=sb_w, tm=tm, tn=tn, name="proj_qkv")
    (u32,) = _proj(xb, [(w_in_b, col_val), (w_in_b, col_gate)], _ep_glu, [F32],
                   n_out=ch, tm=tm, tn=tn, name="proj_glu")

    osb_p = _sb_prompt(qb, kb, vb, batch=bp, seq=tp, heads=SB_HEADS, head_dim=SB_HEAD_DIM, tq=sb_tq)
    osb_s = _sb_sample(qb, kb, vb, cache_sb_k.reshape(bs, -1, SB_HEAD_DIM), cache_sb_v.reshape(bs, -1, SB_HEAD_DIM),
                       row0=mp, batch=bs, t_new=ts, heads=SB_HEADS, head_dim=SB_HEAD_DIM, tk=sb_tk)
    osb = jnp.concatenate([osb_p, osb_s], axis=0)

    u_p = u32[:mp].reshape(bp, tp // conv_tt, conv_tt, ch)
    halo_p = jnp.concatenate([jnp.zeros((bp, 1, CONV_HALO, ch), F32), u_p[:, :-1, conv_tt - CONV_HALO:, :]], axis=1)
    halo_s = jnp.concatenate([jnp.zeros((bs, CONV_HALO - CONV_STATE, ch), F32), state_conv], axis=1)
    conv_args = (conv_w, row(conv_b), row(conv_ln_g), row(conv_ln_b))
    c_p = _conv_module(u32, halo_p.reshape(-1, CONV_HALO, ch), *conv_args, row0=0, n_tiles=mp // conv_tt, tt=conv_tt)
    c_s = _conv_module(u32, halo_s, *conv_args, row0=mp, n_tiles=bs, tt=ts)
    cb = jnp.concatenate([c_p, c_s], axis=0)
    new_conv_p = u32[:mp].reshape(bp, tp, ch)[:, tp - CONV_STATE:, :]
    new_conv_s = jnp.concatenate([state_conv, u32[mp:].reshape(bs, ts, ch)], axis=1)[:, -CONV_STATE:, :]

    merged = _merge(xb, osb, cb, w_in_b, col_ga, col_gb, w_br_a.astype(BF16), w_br_b.astype(BF16),
                    row(b_gate[0]), row(b_gate[1]), tm=tm, tn=tn)
    h1, h1b = _mm_res_ln(merged, w_out.astype(BF16), x_all, row(ln1_g), row(ln1_b), alpha=alpha, tm=tm, tn=tn)

    (qx,) = _proj(h1b, [(xa_wq.astype(BF16), 0)], _ep_copy, [BF16], n_out=xa_w, tm=tm, tn=tn, name="proj_xattn_q")
    mk32, mv32, mkb, mvb = _proj(mem_prompt.reshape(bp * n_mem, d).astype(BF16),
                                 [(xa_wk.astype(BF16), 0), (xa_wv.astype(BF16), 0)], _ep_kv,
                                 [F32, F32, BF16, BF16], n_out=xa_w, tm=tm, tn=tn, name="proj_mem_kv")
    xo_p = _xattn(qx, mkb, mvb, row0=0, batch=bp, seq=tp, tt=xa_tt, heads=XA_HEADS, head_dim=XA_HEAD_DIM)
    xo_s = _xattn(qx, cache_mem_k.reshape(bs * n_mem, xa_w), cache_mem_v.reshape(bs * n_mem, xa_w),
                  row0=mp, batch=bs, seq=ts, tt=ts, heads=XA_HEADS, head_dim=XA_HEAD_DIM)
    xo = jnp.concatenate([xo_p, xo_s], axis=0)
    h2, h2b = _mm_res_ln(xo, xa_wo.astype(BF16), h1, row(ln2_g), row(ln2_b), alpha=alpha, tm=tm, tn=tn)

    idx, w_tok, rank, counts = _router(h2, w_router, row(b_router), tm=ROUTER_TILE_TOKENS)
    plan = _expert_tiles(idx, rank, counts[0], EXPERT_TILE_ROWS)
    h_exp = _expert_up(h2, w_exp_gate, w_exp_up, plan, bm=EXPERT_TILE_ROWS, fc=256)
    y_exp = _expert_down(h_exp, w_exp_down, plan, bm=EXPERT_TILE_ROWS, nc=min(2048, d))
    routed = _combine(y_exp, plan["pos"], w_tok, tt=COMBINE_TILE_TOKENS)
    (sh,) = _proj(h2b, [(w_sh_gate.astype(BF16), 0), (w_sh_up.astype(BF16), 0)], _ep_swiglu, [BF16],
                  n_out=w_sh_gate.shape[1], tm=tm, tn=tn, name="proj_shared_up")
    y, _ = _mm_res_ln(sh, w_sh_down.astype(BF16), h2, row(ln3_g), row(ln3_b), alpha=alpha, tm=tm, tn=tn,
                      extra=routed)

    outs = dict(
        y_p=y[:mp].reshape(bp, tp, d), y_s=y[mp:].reshape(bs, ts, d),
        k_p=k32[:mp].reshape(bp, tp, SB_HEADS, SB_HEAD_DIM), v_p=v32[:mp].reshape(bp, tp, SB_HEADS, SB_HEAD_DIM),
        k_s=k32[mp:].reshape(bs, ts, SB_HEADS, SB_HEAD_DIM), v_s=v32[mp:].reshape(bs, ts, SB_HEADS, SB_HEAD_DIM),
        conv_p=new_conv_p, conv_s=new_conv_s,
        mk_p=mk32.reshape(bp, n_mem, XA_HEADS, XA_HEAD_DIM), mv_p=mv32.reshape(bp, n_mem, XA_HEADS, XA_HEAD_DIM))
    return outs


def kernel(x_prompt, x_sample, mem_prompt, cache_sb_k, cache_sb_v, state_conv, cache_mem_k, cache_mem_v, w_in, b_gate, conv_w, conv_b, conv_ln_g, conv_ln_b, w_br_a, w_br_b, w_out, ln1_g, ln1_b, xa_wq, xa_wk, xa_wv, xa_wo, ln2_g, ln2_b, w_router, b_router, w_exp_gate, w_exp_up, w_exp_down, w_sh_gate, w_sh_up, w_sh_down, ln3_g, ln3_b):
    depth = w_in.shape[0]
    assert depth == 1, "one layer"
    alpha = (2.0 * depth) ** 0.25
    o = _layer(x_prompt, x_sample, mem_prompt, cache_sb_k[0], cache_sb_v[0], state_conv[0], cache_mem_k[0],
               cache_mem_v[0], w_in[0], b_gate[0], conv_w[0], conv_b[0], conv_ln_g[0], conv_ln_b[0], w_br_a[0],
               w_br_b[0], w_out[0], ln1_g[0], ln1_b[0], xa_wq[0], xa_wk[0], xa_wv[0], xa_wo[0], ln2_g[0], ln2_b[0],
               w_router[0], b_router[0], w_exp_gate[0], w_exp_up[0], w_exp_down[0], w_sh_gate[0], w_sh_up[0],
               w_sh_down[0], ln3_g[0], ln3_b[0],
               alpha=alpha, tm=512, tn=512, sb_tq=256, sb_tk=256, conv_tt=128, xa_tt=256)
    stack = lambda a: a[None]
    return (o["y_p"], o["y_s"], stack(o["k_p"]), stack(o["v_p"]), stack(o["conv_p"]), stack(o["mk_p"]),
            stack(o["mv_p"]), stack(o["k_s"]), stack(o["v_s"]), stack(o["conv_s"]))
```

```python
import functools

import jax
import jax.numpy as jnp
from jax import lax
from jax.experimental import pallas as pl
from jax.experimental.pallas import tpu as pltpu

F32 = jnp.float32
BF16 = jnp.bfloat16

V7X_VMEM_LIMIT_BYTES = 56 * 1024 * 1024
LANES = 128

SB_HEADS = 16
SB_HEAD_DIM = 128
XA_HEADS = 4
XA_HEAD_DIM = 256
CONV_WIDTH = 31
CONV_STATE = CONV_WIDTH - 1
CONV_HALO = 32
N_EXPERTS = 256
N_GROUPS = 8
TOPK_GROUPS = 4
TOP_K = 8
ROUTED_SCALE = 2.5
LN_EPS = 1e-5
EXPERT_TILE_ROWS = 384
EXPERT_ROW_GROUP = 128
COMBINE_TILE_TOKENS = 64
ROUTER_TILE_TOKENS = 256


def _params(*sem):
    return pltpu.CompilerParams(dimension_semantics=sem, vmem_limit_bytes=V7X_VMEM_LIMIT_BYTES)


def _dot(a, b):
    return jnp.dot(a, b, preferred_element_type=F32)


def _dot_nt(a, b):
    return lax.dot_general(a, b, (((1,), (1,)), ((), ())), preferred_element_type=F32)


def _proj_kernel(*refs, n_w, n_bias, epilogue):
    x_ref = refs[0]
    w_refs = refs[1:1 + n_w]
    b_refs = refs[1 + n_w:1 + n_w + n_bias]
    o_refs = refs[1 + n_w + n_bias:]
    x = x_ref[...]
    accs = [_dot(x, w[...]) for w in w_refs]
    outs = epilogue(accs, [b[...] for b in b_refs])
    for o_ref, o in zip(o_refs, outs):
        o_ref[...] = o.astype(o_ref.dtype)


def _proj(x, weights, epilogue, out_dtypes, *, n_out, tm, tn, biases=(), name):
    m, k = x.shape
    tn = min(tn, n_out)
    grid = (m // tm, n_out // tn)
    in_specs = [pl.BlockSpec((tm, k), lambda i, j: (i, 0))]
    operands = [x]
    for w, off in weights:
        in_specs.append(pl.BlockSpec((k, tn), functools.partial(lambda i, j, o: (0, o + j), o=off // tn)))
        operands.append(w)
    for b, off in biases:
        in_specs.append(pl.BlockSpec((1, tn), functools.partial(lambda i, j, o: (0, o + j), o=off // tn)))
        operands.append(b)
    out_shape = [jax.ShapeDtypeStruct((m, n_out), dt) for dt in out_dtypes]
    out_specs = [pl.BlockSpec((tm, tn), lambda i, j: (i, j)) for _ in out_dtypes]
    return pl.pallas_call(
        functools.partial(_proj_kernel, n_w=len(weights), n_bias=len(biases), epilogue=epilogue),
        out_shape=out_shape, grid=grid, in_specs=in_specs, out_specs=out_specs,
        compiler_params=_params("parallel", "parallel"), name=name)(*operands)


def _ep_qkv(accs, _):
    q, k, v = accs
    return q, k, v, k, v


def _ep_glu(accs, _):
    val, gate = accs
    return (val * jax.nn.sigmoid(gate),)


def _ep_swiglu(accs, _):
    gate, up = accs
    return (jax.nn.silu(gate) * up,)


def _ep_copy(accs, _):
    return tuple(accs)


def _ep_kv(accs, _):
    k, v = accs
    return k, v, k, v


def _merge_kernel(x_ref, osb_ref, c_ref, wga_ref, wgb_ref, wa_ref, wb_ref, ba_ref, bb_ref, o_ref):
    x = x_ref[...]
    gate_a = jax.nn.sigmoid(_dot(x, wga_ref[...]) + ba_ref[...])
    gate_b = jax.nn.sigmoid(_dot(x, wgb_ref[...]) + bb_ref[...])
    br_a = _dot(osb_ref[...], wa_ref[...])
    br_b = _dot(c_ref[...], wb_ref[...])
    o_ref[...] = (gate_a * br_a + gate_b * br_b).astype(o_ref.dtype)


def _merge(xb, osb, cb, w_in, ga_col, gb_col, w_a, w_b, bias_a, bias_b, *, tm, tn):
    m, d = xb.shape
    ka = osb.shape[1]
    kb = cb.shape[1]
    n = w_a.shape[1]
    return pl.pallas_call(
        _merge_kernel,
        out_shape=jax.ShapeDtypeStruct((m, n), BF16),
        grid=(m // tm, n // tn),
        in_specs=[
            pl.BlockSpec((tm, d), lambda i, j: (i, 0)),
            pl.BlockSpec((tm, ka), lambda i, j: (i, 0)),
            pl.BlockSpec((tm, kb), lambda i, j: (i, 0)),
            pl.BlockSpec((d, tn), lambda i, j: (0, ga_col // tn + j)),
            pl.BlockSpec((d, tn), lambda i, j: (0, gb_col // tn + j)),
            pl.BlockSpec((ka, tn), lambda i, j: (0, j)),
            pl.BlockSpec((kb, tn), lambda i, j: (0, j)),
            pl.BlockSpec((1, tn), lambda i, j: (0, j)),
            pl.BlockSpec((1, tn), lambda i, j: (0, j)),
        ],
        out_specs=pl.BlockSpec((tm, tn), lambda i, j: (i, j)),
        compiler_params=_params("parallel", "parallel"), name="merge_gated_branches",
    )(xb, osb, cb, w_in, w_in, w_a, w_b, bias_a, bias_b)


LN_ROWS = 16


def _layer_norm_rows(x, g, b):
    mu = jnp.mean(x, axis=-1, keepdims=True)
    xc = x - mu
    var = jnp.mean(xc * xc, axis=-1, keepdims=True)
    return xc * lax.rsqrt(var + LN_EPS) * g + b


def _mm_res_ln_kernel(*refs, tn, alpha, has_extra):
    if has_extra:
        a_ref, w_ref, res_ref, extra_ref, g_ref, b_ref, o32_ref, o16_ref = refs
    else:
        a_ref, w_ref, res_ref, g_ref, b_ref, o32_ref, o16_ref = refs
    j = pl.program_id(1)
    n_col_tiles = o32_ref.shape[1] // tn
    pre = alpha * res_ref[...] + _dot(a_ref[...], w_ref[...])
    if has_extra:
        pre = pre + extra_ref[...]
    for jj in range(n_col_tiles):
        @pl.when(j == jj)
        def _():
            o32_ref[:, jj * tn:(jj + 1) * tn] = pre

    @pl.when(j == n_col_tiles - 1)
    def _():
        g = g_ref[...]
        b = b_ref[...]

        def body(r, carry):
            rows = pl.ds(pl.multiple_of(r * LN_ROWS, LN_ROWS), LN_ROWS)
            y = _layer_norm_rows(o32_ref[rows, :], g, b)
            o32_ref[rows, :] = y
            o16_ref[rows, :] = y.astype(o16_ref.dtype)
            return carry

        lax.fori_loop(0, o32_ref.shape[0] // LN_ROWS, body, 0)


def _mm_res_ln(a, w, res, g, b, *, alpha, tm, tn, extra=None):
    m, k = a.shape
    n = w.shape[1]
    in_specs = [
        pl.BlockSpec((tm, k), lambda i, j: (i, 0)),
        pl.BlockSpec((k, tn), lambda i, j: (0, j)),
        pl.BlockSpec((tm, tn), lambda i, j: (i, j)),
    ]
    operands = [a, w, res]
    if extra is not None:
        in_specs.append(pl.BlockSpec((tm, tn), lambda i, j: (i, j)))
        operands.append(extra)
    in_specs += [pl.BlockSpec((1, n), lambda i, j: (0, 0)), pl.BlockSpec((1, n), lambda i, j: (0, 0))]
    operands += [g, b]
    return pl.pallas_call(
        functools.partial(_mm_res_ln_kernel, tn=tn, alpha=alpha, has_extra=extra is not None),
        out_shape=[jax.ShapeDtypeStruct((m, n), F32), jax.ShapeDtypeStruct((m, n), BF16)],
        grid=(m // tm, n // tn), in_specs=in_specs,
        out_specs=[pl.BlockSpec((tm, n), lambda i, j: (i, 0)), pl.BlockSpec((tm, n), lambda i, j: (i, 0))],
        compiler_params=_params("parallel", "arbitrary"), name="matmul_residual_layernorm",
    )(*operands)


def _sb_tile(q, k, v, tri, before, carry, scale):
    z = _dot_nt(q, k) * scale
    softplus = jnp.maximum(z, 0.0) + jnp.log(1.0 + jnp.exp(-jnp.abs(z)))
    log_stay = -softplus if before is None else jnp.where(before, -softplus, 0.0)
    hi = log_stay.astype(BF16)
    lo = (log_stay - hi.astype(F32)).astype(BF16)
    later = _dot(hi, tri) + _dot(lo, tri)
    arg = z + log_stay + later + carry
    if before is not None:
        arg = jnp.where(before, arg, -jnp.inf)
    att = jnp.exp(arg)
    out = _dot(att.astype(BF16), v)
    return out, carry + jnp.sum(log_stay, axis=1, keepdims=True)


def _later_key_matrix(tk):
    r = lax.broadcasted_iota(jnp.int32, (tk, tk), 0)
    c = lax.broadcasted_iota(jnp.int32, (tk, tk), 1)
    return (r > c).astype(BF16)


def _sb_prompt_kernel(q_ref, k_ref, v_ref, tri_ref, o_ref, *, tq, scale):
    qi = pl.program_id(2)
    q = q_ref[...]
    tri = tri_ref[...]
    r = lax.broadcasted_iota(jnp.int32, (tq, tq), 0)
    c = lax.broadcasted_iota(jnp.int32, (tq, tq), 1)
    diag = pl.ds(pl.multiple_of(qi * tq, tq), tq)
    out, carry = _sb_tile(q, k_ref[diag, :], v_ref[diag, :], tri, c < r, jnp.zeros((tq, 1), F32), scale)

    def body(it, state):
        out, carry = state
        rows = pl.ds(pl.multiple_of((qi - 1 - it) * tq, tq), tq)
        o_t, carry = _sb_tile(q, k_ref[rows, :], v_ref[rows, :], tri, None, carry, scale)
        return out + o_t, carry

    out, _ = lax.fori_loop(0, qi, body, (out, carry))
    o_ref[...] = out.astype(o_ref.dtype)


def _sb_prompt(qb, kb, vb, *, batch, seq, heads, head_dim, tq):
    nq = seq // tq
    return pl.pallas_call(
        functools.partial(_sb_prompt_kernel, tq=tq, scale=head_dim ** -0.5),
        out_shape=jax.ShapeDtypeStruct((batch * seq, heads * head_dim), BF16),
        grid=(batch, heads, nq),
        in_specs=[
            pl.BlockSpec((tq, head_dim), lambda b, h, i: (b * nq + i, h)),
            pl.BlockSpec((seq, head_dim), lambda b, h, i: (b, h)),
            pl.BlockSpec((seq, head_dim), lambda b, h, i: (b, h)),
            pl.BlockSpec((tq, tq), lambda b, h, i: (0, 0)),
        ],
        out_specs=pl.BlockSpec((tq, head_dim), lambda b, h, i: (b * nq + i, h)),
        compiler_params=_params("parallel", "parallel", "parallel"), name="stick_breaking_prompt",
    )(qb, kb, vb, _later_key_matrix(tq))


def _sb_sample_kernel(q_ref, kn_ref, vn_ref, kc_ref, vc_ref, tri_new_ref, tri_ref, o_ref, *,
                      heads, head_dim, tk, scale):
    t_new = q_ref.shape[0]
    past = kc_ref.shape[0] // heads
    r = lax.broadcasted_iota(jnp.int32, (t_new, t_new), 0)
    c = lax.broadcasted_iota(jnp.int32, (t_new, t_new), 1)
    before_new = c < r
    tri_new = tri_new_ref[...]
    tri = tri_ref[...]

    for h in range(heads):
        cols = slice(h * head_dim, (h + 1) * head_dim)
        q = q_ref[:, cols]
        out, carry = _sb_tile(q, kn_ref[:, cols], vn_ref[:, cols], tri_new, before_new,
                              jnp.zeros((t_new, 1), F32), scale)
        for t in reversed(range(past // tk)):
            rows = pl.ds(t * tk * heads + h, tk, stride=heads)
            o_t, carry = _sb_tile(q, kc_ref[rows, :].astype(BF16), vc_ref[rows, :].astype(BF16),
                                  tri, None, carry, scale)
            out = out + o_t
        o_ref[:, cols] = out.astype(o_ref.dtype)


def _sb_sample(qb, kb, vb, cache_k, cache_v, *, row0, batch, t_new, heads, head_dim, tk):
    width = heads * head_dim
    blk0 = row0 // t_new
    new_spec = pl.BlockSpec((t_new, width), lambda b: (blk0 + b, 0))
    cache_spec = pl.BlockSpec((None, cache_k.shape[1], head_dim), lambda b: (b, 0, 0))
    return pl.pallas_call(
        functools.partial(_sb_sample_kernel, heads=heads, head_dim=head_dim, tk=tk, scale=head_dim ** -0.5),
        out_shape=jax.ShapeDtypeStruct((batch * t_new, width), BF16),
        grid=(batch,),
        in_specs=[new_spec, new_spec, new_spec, cache_spec, cache_spec,
                  pl.BlockSpec((t_new, t_new), lambda b: (0, 0)), pl.BlockSpec((tk, tk), lambda b: (0, 0))],
        out_specs=pl.BlockSpec((t_new, width), lambda b: (b, 0)),
        compiler_params=_params("parallel"), name="stick_breaking_sample",
    )(qb, kb, vb, cache_k, cache_v, _later_key_matrix(t_new), _later_key_matrix(tk))


CONV_ROWS = 16
CONV_LANES = 512


def _conv_kernel(halo_ref, u_ref, w_ref, cb_ref, g_ref, b_ref, o_ref, win_ref, shift_ref, pre_ref, *, tt):
    ch = u_ref.shape[1]
    win_ref[0:CONV_HALO, :] = halo_ref[...]
    win_ref[CONV_HALO:CONV_HALO + tt, :] = u_ref[...]
    n_shift_rows = shift_ref.shape[1]
    for s in range(1, 8):
        shift_ref[s - 1] = win_ref[s:s + n_shift_rows, :]

    def row_body(rc, carry):
        r0 = pl.multiple_of(rc * CONV_ROWS, CONV_ROWS)
        for lc in range(ch // CONV_LANES):
            lanes = slice(lc * CONV_LANES, (lc + 1) * CONV_LANES)
            acc = jnp.zeros((CONV_ROWS, CONV_LANES), F32)
            for j in range(CONV_WIDTH):
                off = j + CONV_HALO - CONV_STATE
                rows = pl.ds(r0 + (off // 8) * 8, CONV_ROWS)
                if off % 8 == 0:
                    x = win_ref[rows, lanes]
                else:
                    x = shift_ref[off % 8 - 1, rows, lanes]
                acc = acc + x * w_ref[j:j + 1, lanes]
            pre_ref[pl.ds(r0, CONV_ROWS), lanes] = acc + cb_ref[:, lanes]
        y = _layer_norm_rows(pre_ref[pl.ds(r0, CONV_ROWS), :], g_ref[...], b_ref[...])
        o_ref[pl.ds(r0, CONV_ROWS), :] = jax.nn.silu(y).astype(o_ref.dtype)
        return carry

    lax.fori_loop(0, tt // CONV_ROWS, row_body, 0)


def _conv_module(u_all, halo, conv_w, conv_b, ln_g, ln_b, *, row0, n_tiles, tt):
    ch = u_all.shape[1]
    blk0 = row0 // tt
    vec = pl.BlockSpec((1, ch), lambda i: (0, 0))
    return pl.pallas_call(
        functools.partial(_conv_kernel, tt=tt),
        out_shape=jax.ShapeDtypeStruct((n_tiles * tt, ch), BF16),
        grid=(n_tiles,),
        in_specs=[pl.BlockSpec((None, CONV_HALO, ch), lambda i: (i, 0, 0)),
                  pl.BlockSpec((tt, ch), lambda i: (blk0 + i, 0)),
                  pl.BlockSpec((CONV_WIDTH, ch), lambda i: (0, 0)), vec, vec, vec],
        out_specs=pl.BlockSpec((tt, ch), lambda i: (i, 0)),
        scratch_shapes=[pltpu.VMEM((CONV_HALO + tt, ch), F32),
                        pltpu.VMEM((7, tt + CONV_HALO - 8, ch), F32),
                        pltpu.VMEM((tt, ch), F32)],
        compiler_params=_params("parallel"), name="conv_module",
    )(halo, u_all, conv_w, conv_b, ln_g, ln_b)


def _xattn_kernel(q_ref, k_ref, v_ref, o_ref, *, heads, head_dim, scale):
    for h in range(heads):
        cols = slice(h * head_dim, (h + 1) * head_dim)
        s = _dot_nt(q_ref[:, cols], k_ref[:, cols].astype(BF16)) * scale
        e = jnp.exp(s - jnp.max(s, axis=-1, keepdims=True))
        p = e / jnp.sum(e, axis=-1, keepdims=True)
        o_ref[:, cols] = _dot(p.astype(BF16), v_ref[:, cols].astype(BF16)).astype(o_ref.dtype)


def _xattn(qx, mem_k, mem_v, *, row0, batch, seq, tt, heads, head_dim):
    width = heads * head_dim
    n_mem = mem_k.shape[0] // batch
    nt = seq // tt
    blk0 = row0 // tt
    mem_spec = pl.BlockSpec((n_mem, width), lambda b, i: (b, 0))
    return pl.pallas_call(
        functools.partial(_xattn_kernel, heads=heads, head_dim=head_dim, scale=head_dim ** -0.5),
        out_shape=jax.ShapeDtypeStruct((batch * seq, width), BF16),
        grid=(batch, nt),
        in_specs=[pl.BlockSpec((tt, width), lambda b, i: (blk0 + b * nt + i, 0)), mem_spec, mem_spec],
        out_specs=pl.BlockSpec((tt, width), lambda b, i: (b * nt + i, 0)),
        compiler_params=_params("parallel", "parallel"), name="memory_cross_attention",
    )(qx, mem_k, mem_v)


def _split_bf16(x):
    hi = x.astype(BF16)
    return hi, (x - hi.astype(F32)).astype(BF16)


def _first_argmax(x, lane_f):
    m = jnp.max(x, axis=-1, keepdims=True)
    return m, jnp.min(jnp.where(x == m, lane_f, float(x.shape[-1])), axis=-1, keepdims=True)


def _router_kernel(x_ref, w_ref, b_ref, tri_ref, idx_ref, wt_ref, rank_ref, cnt_ref, wh_ref, wl_ref, run_ref):
    @pl.when(pl.program_id(0) == 0)
    def _():
        wh, wl = _split_bf16(w_ref[...])
        wh_ref[...] = wh
        wl_ref[...] = wl
        run_ref[...] = jnp.zeros(run_ref.shape, run_ref.dtype)

    xh, xl = _split_bf16(x_ref[...])
    wh = wh_ref[...]
    logits = _dot(xh, wh) + (_dot(xh, wl_ref[...]) + _dot(xl, wh))
    scores = jax.nn.sigmoid(logits)
    choice = scores + b_ref[...]
    tm, n_e = choice.shape
    lane = lax.broadcasted_iota(jnp.int32, (tm, n_e), 1)
    lane_f = lane.astype(F32)
    group = lax.shift_right_logical(lane, (n_e // N_GROUPS).bit_length() - 1)
    neg = -jnp.inf

    gscore = jnp.zeros((tm, n_e), F32)
    per_group = []
    for g in range(N_GROUPS):
        in_g = group == g
        xg = jnp.where(in_g, choice, neg)
        m1, i1 = _first_argmax(xg, lane_f)
        m2 = jnp.max(jnp.where(lane_f == i1, neg, xg), axis=-1, keepdims=True)
        per_group.append(m1 + m2)
        gscore = jnp.where(in_g, m1 + m2, gscore)
    beaten = jnp.zeros((tm, n_e), F32)
    for g in range(N_GROUPS):
        s = per_group[g]
        wins = (s > gscore) | ((s == gscore) & (group > g))
        beaten = beaten + jnp.where(wins, 1.0, 0.0)
    cand = jnp.where(beaten < float(TOPK_GROUPS), choice, neg)

    col = lax.broadcasted_iota(jnp.int32, (tm, TOP_K), 1)
    idx_f = jnp.zeros((tm, TOP_K), F32)
    wraw = jnp.zeros((tm, TOP_K), F32)
    chosen = jnp.zeros((tm, n_e), F32)
    picks = []
    for k in range(TOP_K):
        _, ik = _first_argmax(cand, lane_f)
        hit = lane_f == ik
        sk = jnp.sum(jnp.where(hit, scores, 0.0), axis=-1, keepdims=True)
        idx_f = jnp.where(col == k, ik, idx_f)
        wraw = jnp.where(col == k, sk, wraw)
        cand = jnp.where(hit, neg, cand)
        chosen = jnp.where(hit, 1.0, chosen)
        picks.append(hit)
    idx_ref[...] = idx_f.astype(jnp.int32)
    wt_ref[...] = wraw / jnp.sum(wraw, axis=-1, keepdims=True) * ROUTED_SCALE

    earlier = _dot(tri_ref[...], chosen.astype(BF16)) + run_ref[...]
    rank_f = jnp.zeros((tm, TOP_K), F32)
    for k in range(TOP_K):
        rk = jnp.sum(jnp.where(picks[k], earlier, 0.0), axis=-1, keepdims=True)
        rank_f = jnp.where(col == k, rk, rank_f)
    rank_ref[...] = rank_f.astype(jnp.int32)
    run_ref[...] = run_ref[...] + jnp.sum(chosen, axis=0, keepdims=True)
    cnt_ref[...] = run_ref[...].astype(jnp.int32)


def _router(x, w, b, *, tm):
    m, d = x.shape
    e = w.shape[1]
    tok = lambda dt: jax.ShapeDtypeStruct((m, TOP_K), dt)
    tok_spec = pl.BlockSpec((tm, TOP_K), lambda i: (i, 0))
    return pl.pallas_call(
        _router_kernel,
        out_shape=[tok(jnp.int32), tok(F32), tok(jnp.int32), jax.ShapeDtypeStruct((1, e), jnp.int32)],
        grid=(m // tm,),
        in_specs=[pl.BlockSpec((tm, d), lambda i: (i, 0)), pl.BlockSpec((d, e), lambda i: (0, 0)),
                  pl.BlockSpec((1, e), lambda i: (0, 0)), pl.BlockSpec((tm, tm), lambda i: (0, 0))],
        out_specs=[tok_spec, tok_spec, tok_spec, pl.BlockSpec((1, e), lambda i: (0, 0))],
        scratch_shapes=[pltpu.VMEM((d, e), BF16), pltpu.VMEM((d, e), BF16), pltpu.VMEM((1, e), F32)],
        compiler_params=_params("arbitrary"), name="router_topk",
    )(x, w, b, _later_key_matrix(tm))


def _expert_tiles(idx, rank, counts, bm):
    t, k = idx.shape
    n_assign = t * k
    n_tiles = n_assign // bm + N_EXPERTS
    starts = jnp.cumsum(counts) - counts
    tiles_e = (counts + bm - 1) // bm
    tile_end = jnp.cumsum(tiles_e)
    tile_first = tile_end - tiles_e
    n_valid = tile_end[-1]
    tid = jnp.arange(n_tiles, dtype=jnp.int32)
    valid = tid < n_valid
    tile_ref = jnp.where(valid, tid, n_valid - 1)
    tile_e = jnp.minimum(jnp.sum(tile_end[None, :] <= tile_ref[:, None], axis=1), N_EXPERTS - 1).astype(jnp.int32)
    r = tile_ref - tile_first[tile_e]
    tile_src = starts[tile_e] + r * bm
    tile_n = jnp.where(valid, jnp.clip(counts[tile_e] - r * bm, 0, bm), 0)
    shift = n_assign.bit_length()
    packed = jnp.sum(jnp.where(idx[:, :, None] == jnp.arange(N_EXPERTS, dtype=jnp.int32),
                               tile_first * (1 << shift) + starts, 0), axis=-1)
    first_of = packed >> shift
    start_of = packed & ((1 << shift) - 1)
    pos = (first_of + rank // bm) * bm + rank % bm
    tok = jnp.broadcast_to(jnp.arange(t, dtype=jnp.int32)[:, None], (t, k))
    tok_sorted = jnp.zeros((n_assign,), jnp.int32).at[(start_of + rank).reshape(-1)].set(
        tok.reshape(-1), unique_indices=True)
    return dict(n_tiles=n_tiles, tok_sorted=tok_sorted, tile_e=tile_e, tile_blk=tile_ref.astype(jnp.int32),
                tile_src=tile_src.astype(jnp.int32), tile_n=tile_n.astype(jnp.int32),
                tile_valid=valid.astype(jnp.int32), pos=pos.reshape(-1).astype(jnp.int32))


def _row_gather_copy(x_hbm, buf, sem, src_row, slot, dst_row):
    return pltpu.make_async_copy(x_hbm.at[pl.ds(src_row, 1), :], buf.at[slot, pl.ds(dst_row, 1), :], sem.at[slot])


def _weight_ring_copy(w_hbm, ring, sem, which, tile_e, step, nf, fc):
    depth = ring.shape[0]
    t = step // nf
    f = step - t * nf
    cols = pl.ds(pl.multiple_of(f * fc, fc), fc)
    return pltpu.make_async_copy(w_hbm.at[tile_e[t], :, cols], ring.at[step % depth], sem.at[which, step % depth])


def _expert_up_kernel(tile_e, tile_blk, tile_src, tile_n, tile_valid, tok_sorted,
                      x_hbm, wg_hbm, wu_hbm, h_ref, xbuf, xb, wg_ring, wu_ring, wgb, wub, sem, wsem):
    t = pl.program_id(0)
    f = pl.program_id(1)
    nf = pl.num_programs(1)
    n_steps = pl.num_programs(0) * nf
    step = t * nf + f
    slot = t % 2
    bm = xb.shape[0]
    fc = wgb.shape[1]
    depth = wg_ring.shape[0]
    share = bm // nf

    def start_weights(s):
        @pl.when(jnp.logical_and(s < n_steps, tile_valid[jnp.minimum(s // nf, pl.num_programs(0) - 1)] > 0))
        def _():
            _weight_ring_copy(wg_hbm, wg_ring, wsem, 0, tile_e, s, nf, fc).start()
            _weight_ring_copy(wu_hbm, wu_ring, wsem, 1, tile_e, s, nf, fc).start()

    def start_gather(tile, dst_slot, lo, hi):
        src = tile_src[tile]

        def body(i, c):
            _row_gather_copy(x_hbm, xbuf, sem, tok_sorted[src + i], dst_slot, i).start()
            return c

        lax.fori_loop(lo, jnp.minimum(hi, tile_n[tile]), body, 0)

    @pl.when(step == 0)
    def _():
        for s in range(depth - 1):
            start_weights(s)
        xbuf[...] = jnp.zeros(xbuf.shape, xbuf.dtype)
        start_gather(0, 0, 0, bm)

    start_weights(step + depth - 1)

    @pl.when(t + 1 < pl.num_programs(0))
    def _():
        start_gather(t + 1, 1 - slot, f * share, (f + 1) * share)

    n = tile_n[t]

    @pl.when(f == 0)
    def _():
        def wait_body(i, c):
            _row_gather_copy(x_hbm, xbuf, sem, 0, slot, i).wait()
            return c

        lax.fori_loop(0, n, wait_body, 0)

        @pl.when(n > 0)
        def _():
            xb[...] = xbuf[slot].astype(BF16)

    @pl.when(n > 0)
    def _():
        _weight_ring_copy(wg_hbm, wg_ring, wsem, 0, tile_e, step, nf, fc).wait()
        _weight_ring_copy(wu_hbm, wu_ring, wsem, 1, tile_e, step, nf, fc).wait()
        wgb[...] = wg_ring[step % depth].astype(BF16)
        wub[...] = wu_ring[step % depth].astype(BF16)
        groups = (n + EXPERT_ROW_GROUP - 1) // EXPERT_ROW_GROUP
        for g in range(1, bm // EXPERT_ROW_GROUP + 1):
            @pl.when(groups == g)
            def _():
                rows = slice(0, g * EXPERT_ROW_GROUP)
                x = xb[rows, :]
                h_ref[rows, :] = (jax.nn.silu(_dot(x, wgb[...])) * _dot(x, wub[...])).astype(h_ref.dtype)


WEIGHT_RING_DEPTH = 3


def _expert_up(x2d, w_gate, w_up, plan, *, bm, fc):
    d = x2d.shape[1]
    f_total = w_gate.shape[2]
    nf = f_total // fc
    n_tiles = plan["n_tiles"]

    def h_map(t, f, te, tb, ts, tn, tv, tok):
        return (tb[t], jnp.where(tv[t] > 0, f, nf - 1))

    any_spec = pl.BlockSpec(memory_space=pl.ANY)
    grid_spec = pltpu.PrefetchScalarGridSpec(
        num_scalar_prefetch=6, grid=(n_tiles, nf),
        in_specs=[any_spec, any_spec, any_spec],
        out_specs=pl.BlockSpec((bm, fc), h_map),
        scratch_shapes=[pltpu.VMEM((2, bm, d), F32), pltpu.VMEM((bm, d), BF16),
                        pltpu.VMEM((WEIGHT_RING_DEPTH, d, fc), F32), pltpu.VMEM((WEIGHT_RING_DEPTH, d, fc), F32),
                        pltpu.VMEM((d, fc), BF16), pltpu.VMEM((d, fc), BF16),
                        pltpu.SemaphoreType.DMA((2,)), pltpu.SemaphoreType.DMA((2, WEIGHT_RING_DEPTH))])
    return pl.pallas_call(
        _expert_up_kernel, grid_spec=grid_spec,
        out_shape=jax.ShapeDtypeStruct((n_tiles * bm, f_total), BF16),
        compiler_params=_params("arbitrary", "arbitrary"), name="routed_experts_up",
    )(plan["tile_e"], plan["tile_blk"], plan["tile_src"], plan["tile_n"], plan["tile_valid"],
      plan["tok_sorted"], x2d, w_gate, w_up)


def _expert_down_kernel(tile_e, tile_blk, tile_n, tile_valid, h_ref, wd_ref, y_ref, wdb):
    n = tile_n[pl.program_id(0)]

    @pl.when(n > 0)
    def _():
        wdb[...] = wd_ref[...].astype(BF16)
        groups = (n + EXPERT_ROW_GROUP - 1) // EXPERT_ROW_GROUP
        for g in range(1, h_ref.shape[0] // EXPERT_ROW_GROUP + 1):
            @pl.when(groups == g)
            def _():
                rows = slice(0, g * EXPERT_ROW_GROUP)
                y_ref[rows, :] = _dot(h_ref[rows, :], wdb[...])


def _expert_down(h, w_down, plan, *, bm, nc):
    f_total = h.shape[1]
    d = w_down.shape[2]
    n_cols = d // nc
    n_tiles = plan["n_tiles"]

    def col(c, tv, t):
        return jnp.where(tv[t] > 0, c, n_cols - 1)

    grid_spec = pltpu.PrefetchScalarGridSpec(
        num_scalar_prefetch=4, grid=(n_tiles, n_cols),
        in_specs=[pl.BlockSpec((bm, f_total), lambda t, c, te, tb, tn, tv: (tb[t], 0)),
                  pl.BlockSpec((None, f_total, nc), lambda t, c, te, tb, tn, tv: (te[t], 0, col(c, tv, t)))],
        out_specs=pl.BlockSpec((bm, nc), lambda t, c, te, tb, tn, tv: (tb[t], col(c, tv, t))),
        scratch_shapes=[pltpu.VMEM((f_total, nc), BF16)])
    return pl.pallas_call(
        _expert_down_kernel, grid_spec=grid_spec,
        out_shape=jax.ShapeDtypeStruct((n_tiles * bm, d), F32),
        compiler_params=_params("arbitrary", "arbitrary"), name="routed_experts_down",
    )(plan["tile_e"], plan["tile_blk"], plan["tile_n"], plan["tile_valid"], h, w_down)


COMBINE_ROWS = 8


def _combine_kernel(pos, y_hbm, w_ref, o_ref, buf, sem, *, tt, top_k):
    i = pl.program_id(0)
    slot = i % 2

    def start_gather(tile, dst_slot):
        def body(j, c):
            base = (tile * tt + j) * top_k
            for k in range(top_k):
                pltpu.make_async_copy(y_hbm.at[pl.ds(pos[base + k], 1), :],
                                      buf.at[dst_slot, pl.ds(k * tt + j, 1), :], sem.at[dst_slot]).start()
            return c

        lax.fori_loop(0, tt, body, 0)

    @pl.when(i == 0)
    def _():
        start_gather(0, 0)

    @pl.when(i + 1 < pl.num_programs(0))
    def _():
        start_gather(i + 1, 1 - slot)

    pltpu.make_async_copy(y_hbm.at[pl.ds(0, top_k * tt), :], buf.at[slot], sem.at[slot]).wait()

    def row_body(rc, c):
        r0 = pl.multiple_of(rc * COMBINE_ROWS, COMBINE_ROWS)
        w = w_ref[pl.ds(r0, COMBINE_ROWS), :]
        acc = jnp.zeros((COMBINE_ROWS, o_ref.shape[1]), F32)
        for k in range(top_k):
            acc = acc + w[:, k:k + 1] * buf[slot, pl.ds(k * tt + r0, COMBINE_ROWS), :]
        o_ref[pl.ds(r0, COMBINE_ROWS), :] = acc
        return c

    lax.fori_loop(0, tt // COMBINE_ROWS, row_body, 0)


def _combine(y, pos, w, *, tt):
    t, top_k = w.shape
    d = y.shape[1]
    grid_spec = pltpu.PrefetchScalarGridSpec(
        num_scalar_prefetch=1, grid=(t // tt,),
        in_specs=[pl.BlockSpec(memory_space=pl.ANY), pl.BlockSpec((tt, top_k), lambda i, p: (i, 0))],
        out_specs=pl.BlockSpec((tt, d), lambda i, p: (i, 0)),
        scratch_shapes=[pltpu.VMEM((2, top_k * tt, d), F32), pltpu.SemaphoreType.DMA((2,))])
    return pl.pallas_call(
        functools.partial(_combine_kernel, tt=tt, top_k=top_k), grid_spec=grid_spec,
        out_shape=jax.ShapeDtypeStruct((t, d), F32),
        compiler_params=_params("arbitrary"), name="routed_experts_combine",
    )(pos, y, w)


def _layer(x_prompt, x_sample, mem_prompt, cache_sb_k, cache_sb_v, state_conv, cache_mem_k, cache_mem_v,
           w_in, b_gate, conv_w, conv_b, conv_ln_g, conv_ln_b, w_br_a, w_br_b, w_out, ln1_g, ln1_b,
           xa_wq, xa_wk, xa_wv, xa_wo, ln2_g, ln2_b, w_router, b_router, w_exp_gate, w_exp_up, w_exp_down,
           w_sh_gate, w_sh_up, w_sh_down, ln3_g, ln3_b, *, alpha, tm, tn, sb_tq, sb_tk, conv_tt, xa_tt):
    bp, tp, d = x_prompt.shape
    bs, ts, _ = x_sample.shape
    mp, ms = bp * tp, bs * ts
    sb_w = SB_HEADS * SB_HEAD_DIM
    ch = conv_w.shape[1]
    xa_w = XA_HEADS * XA_HEAD_DIM
    n_mem = mem_prompt.shape[1]
    col_val, col_gate, col_ga, col_gb = 3 * sb_w, 3 * sb_w + ch, 3 * sb_w + 2 * ch, 3 * sb_w + 2 * ch + d
    row = lambda v: v.reshape(1, -1)

    x_all = jnp.concatenate([x_prompt.reshape(mp, d), x_sample.reshape(ms, d)], axis=0)
    xb = x_all.astype(BF16)
    w_in_b = w_in.astype(BF16)

    qb, k32, v32, kb, vb = _proj(xb, [(w_in_b, 0), (w_in_b, sb_w), (w_in_b, 2 * sb_w)], _ep_qkv,
                                 [BF16, F32, F32, BF16, BF16], n_out=sb_w, tm=tm, tn=tn, name="proj_qkv")
    (u32,) = _proj(xb, [(w_in_b, col_val), (w_in_b, col_gate)], _ep_glu, [F32],
                   n_out=ch, tm=tm, tn=tn, name="proj_glu")

    osb_p = _sb_prompt(qb, kb, vb, batch=bp, seq=tp, heads=SB_HEADS, head_dim=SB_HEAD_DIM, tq=sb_tq)
    osb_s = _sb_sample(qb, kb, vb, cache_sb_k.reshape(bs, -1, SB_HEAD_DIM), cache_sb_v.reshape(bs, -1, SB_HEAD_DIM),
                       row0=mp, batch=bs, t_new=ts, heads=SB_HEADS, head_dim=SB_HEAD_DIM, tk=sb_tk)
    osb = jnp.concatenate([osb_p, osb_s], axis=0)

    u_p = u32[:mp].reshape(bp, tp // conv_tt, conv_tt, ch)
    halo_p = jnp.concatenate([jnp.zeros((bp, 1, CONV_HALO, ch), F32), u_p[:, :-1, conv_tt - CONV_HALO:, :]], axis=1)
    halo_s = jnp.concatenate([jnp.zeros((bs, CONV_HALO - CONV_STATE, ch), F32), state_conv], axis=1)
    conv_args = (conv_w, row(conv_b), row(conv_ln_g), row(conv_ln_b))
    c_p = _conv_module(u32, halo_p.reshape(-1, CONV_HALO, ch), *conv_args, row0=0, n_tiles=mp // conv_tt, tt=conv_tt)
    c_s = _conv_module(u32, halo_s, *conv_args, row0=mp, n_tiles=bs, tt=ts)
    cb = jnp.concatenate([c_p, c_s], axis=0)
    new_conv_p = u32[:mp].reshape(bp, tp, ch)[:, tp - CONV_STATE:, :]
    new_conv_s = jnp.concatenate([state_conv, u32[mp:].reshape(bs, ts, ch)], axis=1)[:, -CONV_STATE:, :]

    merged = _merge(xb, osb, cb, w_in_b, col_ga, col_gb, w_br_a.astype(BF16), w_br_b.astype(BF16),
                    row(b_gate[0]), row(b_gate[1]), tm=tm, tn=tn)
    h1, h1b = _mm_res_ln(merged, w_out.astype(BF16), x_all, row(ln1_g), row(ln1_b), alpha=alpha, tm=tm, tn=tn)

    (qx,) = _proj(h1b, [(xa_wq.astype(BF16), 0)], _ep_copy, [BF16], n_out=xa_w, tm=tm, tn=tn, name="proj_xattn_q")
    mk32, mv32, mkb, mvb = _proj(mem_prompt.reshape(bp * n_mem, d).astype(BF16),
                                 [(xa_wk.astype(BF16), 0), (xa_wv.astype(BF16), 0)], _ep_kv,
                                 [F32, F32, BF16, BF16], n_out=xa_w, tm=tm, tn=tn, name="proj_mem_kv")
    xo_p = _xattn(qx, mkb, mvb, row0=0, batch=bp, seq=tp, tt=xa_tt, heads=XA_HEADS, head_dim=XA_HEAD_DIM)
    xo_s = _xattn(qx, cache_mem_k.reshape(bs * n_mem, xa_w), cache_mem_v.reshape(bs * n_mem, xa_w),
                  row0=mp, batch=bs, seq=ts, tt=ts, heads=XA_HEADS, head_dim=XA_HEAD_DIM)
    xo = jnp.concatenate([xo_p, xo_s], axis=0)
    h2, h2b = _mm_res_ln(xo, xa_wo.astype(BF16), h1, row(ln2_g), row(ln2_b), alpha=alpha, tm=tm, tn=tn)

    idx, w_tok, rank, counts = _router(h2, w_router, row(b_router), tm=ROUTER_TILE_TOKENS)
    plan = _expert_tiles(idx, rank, counts[0], EXPERT_TILE_ROWS)
    h_exp = _expert_up(h2, w_exp_gate, w_exp_up, plan, bm=EXPERT_TILE_ROWS, fc=256)
    y_exp = _expert_down(h_exp, w_exp_down, plan, bm=EXPERT_TILE_ROWS, nc=min(2048, d))
    routed = _combine(y_exp, plan["pos"], w_tok, tt=COMBINE_TILE_TOKENS)
    (sh,) = _proj(h2b, [(w_sh_gate.astype(BF16), 0), (w_sh_up.astype(BF16), 0)], _ep_swiglu, [BF16],
                  n_out=w_sh_gate.shape[1], tm=tm, tn=tn, name="proj_shared_up")
    y, _ = _mm_res_ln(sh, w_sh_down.astype(BF16), h2, row(ln3_g), row(ln3_b), alpha=alpha, tm=tm, tn=tn,
                      extra=routed)

    outs = dict(
        y_p=y[:mp].reshape(bp, tp, d), y_s=y[mp:].reshape(bs, ts, d),
        k_p=k32[:mp].reshape(bp, tp, SB_HEADS, SB_HEAD_DIM), v_p=v32[:mp].reshape(bp, tp, SB_HEADS, SB_HEAD_DIM),
        k_s=k32[mp:].reshape(bs, ts, SB_HEADS, SB_HEAD_DIM), v_s=v32[mp:].reshape(bs, ts, SB_HEADS, SB_HEAD_DIM),
        conv_p=new_conv_p, conv_s=new_conv_s,
        mk_p=mk32.reshape(bp, n_mem, XA_HEADS, XA_HEAD_DIM), mv_p=mv32.reshape(bp, n_mem, XA_HEADS, XA_HEAD_DIM))
    return outs


def kernel(x_prompt, x_sample, mem_prompt, cache_sb_k, cache_sb_v, state_conv, cache_mem_k, cache_mem_v, w_in, b_gate, conv_w, conv_b, conv_ln_g, conv_ln_b, w_br_a, w_br_b, w_out, ln1_g, ln1_b, xa_wq, xa_wk, xa_wv, xa_wo, ln2_g, ln2_b, w_router, b_router, w_exp_gate, w_exp_up, w_exp_down, w_sh_gate, w_sh_up, w_sh_down, ln3_g, ln3_b):
    depth = w_in.shape[0]
    assert depth == 1, "one layer"
    alpha = (2.0 * depth) ** 0.25
    o = _layer(x_prompt, x_sample, mem_prompt, cache_sb_k[0], cache_sb_v[0], state_conv[0], cache_mem_k[0],
               cache_mem_v[0], w_in[0], b_gate[0], conv_w[0], conv_b[0], conv_ln_g[0], conv_ln_b[0], w_br_a[0],
               w_br_b[0], w_out[0], ln1_g[0], ln1_b[0], xa_wq[0], xa_wk[0], xa_wv[0], xa_wo[0], ln2_g[0], ln2_b[0],
               w_router[0], b_router[0], w_exp_gate[0], w_exp_up[0], w_exp_down[0], w_sh_gate[0], w_sh_up[0],
               w_sh_down[0], ln3_g[0], ln3_b[0],
               alpha=alpha, tm=512, tn=512, sb_tq=256, sb_tk=256, conv_tt=128, xa_tt=256)
    stack = lambda a: a[None]
    return (o["y_p"], o["y_s"], stack(o["k_p"]), stack(o["v_p"]), stack(o["conv_p"]), stack(o["mk_p"]),
            stack(o["mv_p"]), stack(o["k_s"]), stack(o["v_s"]), stack(o["conv_s"]))
```
